```python
import math
import jax
import jax.numpy as jnp
from jax import lax
import numpy as np

D_MODEL = 1024
BATCH = 8
SEQ = 4096
DEPTH = 1
DEC_BATCH = 128
DEC_SEQ = 4
PAST_LEN = 8192
PAGE_SIZE = 128

ATT_HEAD_DIM = 64
ATT_HEADS = (D_MODEL // 2) // ATT_HEAD_DIM
ATT_KV_HEADS = ATT_HEADS // 2
ATT_GROUP = ATT_HEADS // ATT_KV_HEADS
MOBA_BLOCK = 256
MOBA_TOPK = 3
Q_CHUNK = 128
REL_BUCKETS = 32
REL_MAX_DIST = 4096
GDN_DK = 128
GDN_DV = 128
GDN_HEADS = (D_MODEL // 2) // GDN_DV
GDN_CONV = 4
GDN_CHUNK = 64
ATT_Q_W = ATT_HEADS * ATT_HEAD_DIM
ATT_KV_W = ATT_KV_HEADS * ATT_HEAD_DIM
GDN_QK_W = GDN_HEADS * GDN_DK
GDN_V_W = GDN_HEADS * GDN_DV
GDN_CONV_W = 2 * GDN_QK_W + GDN_V_W
MIX_W = ATT_Q_W + GDN_V_W
IN_W = ATT_Q_W + 2 * ATT_KV_W + GDN_CONV_W + GDN_V_W + 2 * GDN_HEADS
N_EXPERTS = 32
TOP_K = 4
D_FF = D_MODEL
SWIGLU_LIMIT = 7.0
SWIGLU_ALPHA = 1.702
EPS = 1e-6
NEG = -1e30

kernel_name = 'hymba_moba_gdn_moe_adaln_step'


def rms_norm(x, w):
    xf = x.astype(jnp.float32)
    y = xf * lax.rsqrt(jnp.mean(xf * xf, axis=-1, keepdims=True) + EPS)
    return (y * w).astype(x.dtype)


def l2_normalize(x):
    return x * lax.rsqrt(jnp.sum(x * x, axis=-1, keepdims=True) + EPS)


def ada_modulation(c, w_ada, b_ada):
    m = jax.nn.silu(c) @ w_ada + b_ada
    return jnp.split(m[:, None, :], 6, axis=-1)


def rel_bucket(dist):
    n = jnp.maximum(dist, 0)
    max_exact = REL_BUCKETS // 2
    nf = jnp.maximum(n, max_exact).astype(jnp.float32)
    large = max_exact + (jnp.log(nf / max_exact) / math.log(REL_MAX_DIST / max_exact)
                         * (REL_BUCKETS - max_exact)).astype(jnp.int32)
    return jnp.where(n < max_exact, n, jnp.minimum(large, REL_BUCKETS - 1))


def moba_attend(q, qpos, k, v, rel_bias):
    nq = q.shape[0]
    n_blk = k.shape[0] // MOBA_BLOCK
    kk = min(MOBA_TOPK, n_blk)
    scale = ATT_HEAD_DIM ** -0.5
    qg = q.reshape(nq, ATT_KV_HEADS, ATT_GROUP, ATT_HEAD_DIM).astype(jnp.float32)
    kb = k.reshape(n_blk, MOBA_BLOCK, ATT_KV_HEADS, ATT_HEAD_DIM).transpose(2, 0, 1, 3)
    vb = v.reshape(n_blk, MOBA_BLOCK, ATT_KV_HEADS, ATT_HEAD_DIM).transpose(2, 0, 1, 3)
    k_mean = jnp.mean(kb.astype(jnp.float32), axis=2)
    own_blk = qpos // MOBA_BLOCK
    gate = jnp.einsum('qhgd,hnd->qhgn', qg, k_mean)
    past = jnp.arange(n_blk)[None, None, None, :] < own_blk[:, None, None, None]
    _, sel = lax.top_k(jnp.where(past, gate, NEG), kk)
    sel_ok = jnp.arange(kk)[None, None, None, :] < own_blk[:, None, None, None]
    hh = jnp.arange(ATT_KV_HEADS)[None, :, None, None]
    k_sel = kb[hh, sel].astype(jnp.float32)
    v_sel = vb[hh, sel].astype(jnp.float32)
    tab = rel_bias.T.reshape(ATT_KV_HEADS, ATT_GROUP, REL_BUCKETS).astype(jnp.float32)
    kpos_sel = sel[..., None] * MOBA_BLOCK + jnp.arange(MOBA_BLOCK)
    b_sel = rel_bucket(qpos[:, None, None, None, None] - kpos_sel)
    bias_sel = tab[hh[..., None], jnp.arange(ATT_GROUP)[None, None, :, None, None], b_sel]
    s_sel = jnp.einsum('qhgd,qhgjtd->qhgjt', qg, k_sel) * scale + bias_sel
    s_sel = jnp.where(sel_ok[..., None], s_sel, NEG).reshape(nq, ATT_KV_HEADS, ATT_GROUP, kk * MOBA_BLOCK)
    k_own = kb[:, own_blk].astype(jnp.float32)
    v_own = vb[:, own_blk].astype(jnp.float32)
    kpos_own = (own_blk * MOBA_BLOCK)[:, None] + jnp.arange(MOBA_BLOCK)
    bias_own = tab[:, :, rel_bucket(qpos[:, None] - kpos_own)].transpose(2, 0, 1, 3)
    s_own = jnp.einsum('qhgd,hqtd->qhgt', qg, k_own) * scale + bias_own
    s_own = jnp.where((kpos_own <= qpos[:, None])[:, None, None, :], s_own, NEG)
    p = jax.nn.softmax(jnp.concatenate([s_sel, s_own], axis=-1), axis=-1)
    p_sel = p[..., :kk * MOBA_BLOCK].reshape(nq, ATT_KV_HEADS, ATT_GROUP, kk, MOBA_BLOCK)
    p_own = p[..., kk * MOBA_BLOCK:]
    out = (jnp.einsum('qhgjt,qhgjtd->qhgd', p_sel, v_sel)
           + jnp.einsum('qhgt,hqtd->qhgd', p_own, v_own))
    return out.reshape(nq, ATT_Q_W).astype(q.dtype)


def moba_prompt(q, k, v, rel_bias):
    b, seq = q.shape[0], q.shape[1]
    l_pad = -(-seq // MOBA_BLOCK) * MOBA_BLOCK
    pad = ((0, 0), (0, l_pad - seq), (0, 0), (0, 0))
    kp, vp = jnp.pad(k, pad), jnp.pad(v, pad)
    n_chunk = seq // Q_CHUNK
    qc = q.reshape(b, n_chunk, Q_CHUNK, ATT_HEADS, ATT_HEAD_DIM)

    def per_seq(args):
        q_s, k_s, v_s = args

        def per_chunk(args2):
            q_c, c_idx = args2
            qpos = c_idx * Q_CHUNK + jnp.arange(Q_CHUNK, dtype=jnp.int32)
            return moba_attend(q_c, qpos, k_s, v_s, rel_bias)
        return lax.map(per_chunk, (q_s, jnp.arange(n_chunk, dtype=jnp.int32)))

    out = lax.map(per_seq, (qc, kp, vp))
    return out.reshape(b, seq, ATT_Q_W)


def moba_sample(q, k_new, v_new, cache_k, cache_v, layer, page_table, rel_bias):
    dec_seq = q.shape[1]
    total = PAST_LEN + dec_seq
    l_pad = -(-total // MOBA_BLOCK) * MOBA_BLOCK
    qpos = PAST_LEN + jnp.arange(dec_seq, dtype=jnp.int32)

    def per_seq(args):
        q_s, kn, vn, pages = args
        pad = jnp.zeros((l_pad - total, ATT_KV_HEADS, ATT_HEAD_DIM), kn.dtype)
        k_past = cache_k[layer, pages].reshape(PAST_LEN, ATT_KV_HEADS, ATT_HEAD_DIM).astype(kn.dtype)
        v_past = cache_v[layer, pages].reshape(PAST_LEN, ATT_KV_HEADS, ATT_HEAD_DIM).astype(vn.dtype)
        k_s = jnp.concatenate([k_past, kn, pad], axis=0)
        v_s = jnp.concatenate([v_past, vn, pad], axis=0)
        return moba_attend(q_s, qpos, k_s, v_s, rel_bias)

    return lax.map(per_seq, (q, k_new, v_new, page_table))


def causal_conv(x_ext, w, t_out):
    return sum(x_ext[:, i:i + t_out] * w[i] for i in range(GDN_CONV))


def gdn_features(conv_out, b_raw, a_raw, a_log, dt_bias):
    n, t, _ = conv_out.shape
    act = jax.nn.silu(conv_out.astype(jnp.float32))
    q = act[..., :GDN_QK_W].reshape(n, t, GDN_HEADS, GDN_DK)
    k = act[..., GDN_QK_W:2 * GDN_QK_W].reshape(n, t, GDN_HEADS, GDN_DK)
    v = act[..., 2 * GDN_QK_W:].reshape(n, t, GDN_HEADS, GDN_DV)
    q = l2_normalize(q) * GDN_DK ** -0.5
    k = l2_normalize(k)
    beta = jax.nn.sigmoid(b_raw.astype(jnp.float32))
    g = -jnp.exp(a_log.astype(jnp.float32)) * jax.nn.softplus(a_raw.astype(jnp.float32) + dt_bias.astype(jnp.float32))
    return q, k, v, beta, g


def gdn_chunked(q, k, v, beta, g):
    b, t = q.shape[0], q.shape[1]
    n = t // GDN_CHUNK
    cs = GDN_CHUNK

    def to_chunks(x):
        return x.reshape((b, n, cs) + x.shape[2:]).swapaxes(2, 3)

    qc, kc, vc, bc, gc = (to_chunks(a) for a in (q, k, v, beta, g))
    gcum = jnp.cumsum(gc, axis=-1)
    tril = jnp.tril(jnp.ones((cs, cs), bool))
    strict = jnp.tril(jnp.ones((cs, cs), bool), -1)
    diff = gcum[..., :, None] - gcum[..., None, :]
    decay = jnp.where(tril, jnp.exp(jnp.where(tril, diff, 0.0)), 0.0)
    kbeta = kc * bc[..., None]
    lower = jnp.where(strict, jnp.einsum('bnhid,bnhjd->bnhij', kbeta, kc) * decay, 0.0)
    eye = jnp.eye(cs, dtype=lower.dtype)
    tmat = lax.linalg.triangular_solve(lower + eye, jnp.broadcast_to(eye, lower.shape),
                                       left_side=True, lower=True, unit_diagonal=True)
    u = tmat @ (vc * bc[..., None])
    w = tmat @ (kbeta * jnp.exp(gcum)[..., None])
    intra = jnp.where(tril, jnp.einsum('bnhid,bnhjd->bnhij', qc, kc) * decay, 0.0)

    def step(s, xs):
        q_i, k_i, u_i, w_i, g_i, a_i = xs
        v_new = u_i - jnp.einsum('bhck,bhkv->bhcv', w_i, s)
        o_i = (jnp.einsum('bhck,bhkv->bhcv', q_i * jnp.exp(g_i)[..., None], s)
               + jnp.einsum('bhcj,bhjv->bhcv', a_i, v_new))
        g_last = g_i[..., -1:]
        s = s * jnp.exp(g_last)[..., None] + jnp.einsum(
            'bhck,bhcv->bhkv', k_i * jnp.exp(g_last - g_i)[..., None], v_new)
        return s, o_i

    s0 = jnp.zeros((b, GDN_HEADS, GDN_DK, GDN_DV), jnp.float32)
    xs = tuple(jnp.moveaxis(a, 1, 0) for a in (qc, kc, u, w, gcum, intra))
    s_fin, o = lax.scan(step, s0, xs)
    o = jnp.moveaxis(o, 0, 1).swapaxes(2, 3).reshape(b, t, GDN_HEADS, GDN_DV)
    return o, s_fin


def gdn_recurrent(q, k, v, beta, g, s0):
    def step(s, xs):
        q_t, k_t, v_t, b_t, g_t = xs
        s = s * jnp.exp(g_t)[..., None, None]
        delta = (v_t - jnp.einsum('bhk,bhkv->bhv', k_t, s)) * b_t[..., None]
        s = s + jnp.einsum('bhk,bhv->bhkv', k_t, delta)
        return s, jnp.einsum('bhk,bhkv->bhv', q_t, s)

    s_fin, o = lax.scan(step, s0, tuple(a.swapaxes(0, 1) for a in (q, k, v, beta, g)))
    return o.swapaxes(0, 1), s_fin


def gdn_prompt(qkv_pre, b_raw, a_raw, conv_w, a_log, dt_bias):
    t = qkv_pre.shape[1]
    ext = jnp.pad(qkv_pre, ((0, 0), (GDN_CONV - 1, 0), (0, 0)))
    q, k, v, beta, g = gdn_features(causal_conv(ext, conv_w, t), b_raw, a_raw, a_log, dt_bias)
    o, s = gdn_chunked(q, k, v, beta, g)
    return o, qkv_pre[:, t - (GDN_CONV - 1):], s


def gdn_sample(qkv_pre, b_raw, a_raw, conv_w, a_log, dt_bias, conv_state, ssm_state):
    t = qkv_pre.shape[1]
    ext = jnp.concatenate([conv_state.astype(qkv_pre.dtype), qkv_pre], axis=1)
    q, k, v, beta, g = gdn_features(causal_conv(ext, conv_w, t), b_raw, a_raw, a_log, dt_bias)
    o, s = gdn_recurrent(q, k, v, beta, g, ssm_state.astype(jnp.float32))
    return o, ext[:, t:], s


def gated_rms_norm(o, z, w):
    y = o * lax.rsqrt(jnp.mean(o * o, axis=-1, keepdims=True) + EPS) * w
    return y * jax.nn.silu(z.astype(jnp.float32))


def moe_ffn(h, w_router, b_router, w_up, b_up, w_down, b_down):
    shp = h.shape
    tok = h.reshape(-1, shp[-1])
    logits = (tok @ w_router + b_router).astype(jnp.float32)
    top_val, top_idx = lax.top_k(logits, TOP_K)
    top_w = jax.nn.softmax(top_val, axis=-1)
    gates = jnp.sum(jax.nn.one_hot(top_idx, N_EXPERTS, dtype=jnp.float32) * top_w[..., None], axis=-2)

    def expert_step(acc, ew):
        w1, b1, w2, b2, g_e = ew
        hu = tok @ w1 + b1
        x_glu = jnp.minimum(hu[:, :D_FF], SWIGLU_LIMIT)
        x_lin = jnp.clip(hu[:, D_FF:], -SWIGLU_LIMIT, SWIGLU_LIMIT)
        act = x_glu * jax.nn.sigmoid(SWIGLU_ALPHA * x_glu) * (x_lin + 1.0)
        return acc + g_e[:, None] * (act @ w2 + b2).astype(jnp.float32), None

    acc, _ = lax.scan(expert_step, jnp.zeros(tok.shape, jnp.float32), (w_up, b_up, w_down, b_down, gates.T))
    return acc.reshape(shp).astype(h.dtype)


def layer_forward(x, c, attn_fn, gdn_fn, w_ada, b_ada, norm_attn_w, norm_ffn_w, w_in, w_out,
                  gdn_norm_w, w_router, b_router, w_up, b_up, w_down, b_down):
    n, t, _ = x.shape
    sh_a, sc_a, gt_a, sh_f, sc_f, gt_f = ada_modulation(c, w_ada, b_ada)
    h = rms_norm(x, norm_attn_w) * (1.0 + sc_a) + sh_a
    proj = h @ w_in
    o1 = ATT_Q_W
    o2 = o1 + ATT_KV_W
    o3 = o2 + ATT_KV_W
    o4 = o3 + GDN_CONV_W
    o5 = o4 + GDN_V_W
    o6 = o5 + GDN_HEADS
    q_a = proj[..., :o1].reshape(n, t, ATT_HEADS, ATT_HEAD_DIM)
    k_a = proj[..., o1:o2].reshape(n, t, ATT_KV_HEADS, ATT_HEAD_DIM)
    v_a = proj[..., o2:o3].reshape(n, t, ATT_KV_HEADS, ATT_HEAD_DIM)
    z_g = proj[..., o4:o5].reshape(n, t, GDN_HEADS, GDN_DV)
    attn_out = attn_fn(q_a, k_a, v_a)
    o_g, conv_new, ssm_new = gdn_fn(proj[..., o3:o4], proj[..., o5:o6], proj[..., o6:])
    gdn_out = gated_rms_norm(o_g, z_g, gdn_norm_w).reshape(n, t, GDN_V_W).astype(x.dtype)
    x = x + gt_a * (jnp.concatenate([attn_out, gdn_out], axis=-1) @ w_out)
    h2 = rms_norm(x, norm_ffn_w) * (1.0 + sc_f) + sh_f
    x = x + gt_f * moe_ffn(h2, w_router, b_router, w_up, b_up, w_down, b_down)
    return x, k_a, v_a, conv_new, ssm_new


def setup_inputs(seed: int = 0) -> dict:
    key = jax.random.key(seed)
    ks = jax.random.split(key, 32)
    n_pages = PAST_LEN // PAGE_SIZE
    n_pool = (DEC_BATCH * n_pages * 5) // 4

    def nrm(k, shape, s):
        return jax.random.normal(k, shape, jnp.float32) * s

    perm = jax.random.permutation(ks[6], n_pool)
    return {
        'x_prompt': nrm(ks[0], (BATCH, SEQ, D_MODEL), 1.0),
        'x_sample': nrm(ks[1], (DEC_BATCH, DEC_SEQ, D_MODEL), 1.0),
        'cache_k': nrm(ks[2], (DEPTH, n_pool, PAGE_SIZE, ATT_KV_HEADS, ATT_HEAD_DIM), 1.0),
        'cache_v': nrm(ks[3], (DEPTH, n_pool, PAGE_SIZE, ATT_KV_HEADS, ATT_HEAD_DIM), 1.0),
        'state_conv': nrm(ks[4], (DEPTH, DEC_BATCH, GDN_CONV - 1, GDN_CONV_W), 1.0),
        'state_ssm': nrm(ks[5], (DEPTH, DEC_BATCH, GDN_HEADS, GDN_DK, GDN_DV), 0.05),
        'page_table': perm[:DEC_BATCH * n_pages].reshape(DEC_BATCH, n_pages).astype(jnp.int32),
        'c_prompt': nrm(ks[7], (BATCH, D_MODEL), 1.0),
        'c_sample': nrm(ks[8], (DEC_BATCH, D_MODEL), 1.0),
        'w_ada': nrm(ks[9], (DEPTH, D_MODEL, 6 * D_MODEL), 0.5 * D_MODEL ** -0.5),
        'b_ada': nrm(ks[10], (DEPTH, 6 * D_MODEL), 0.01),
        'norm_attn_w': 1.0 + nrm(ks[11], (DEPTH, D_MODEL), 0.05),
        'norm_ffn_w': 1.0 + nrm(ks[12], (DEPTH, D_MODEL), 0.05),
        'norm_final_w': 1.0 + nrm(ks[13], (D_MODEL,), 0.05),
        'w_in': nrm(ks[14], (DEPTH, D_MODEL, IN_W), D_MODEL ** -0.5),
        'rel_bias': nrm(ks[15], (REL_BUCKETS, ATT_HEADS), 0.5),
        'conv_w': nrm(ks[16], (DEPTH, GDN_CONV, GDN_CONV_W), GDN_CONV ** -0.5),
        'a_log': jnp.log(jax.random.uniform(ks[17], (DEPTH, GDN_HEADS), jnp.float32, 1.0, 16.0)),
        'dt_bias': nrm(ks[18], (DEPTH, GDN_HEADS), 0.5),
        'gdn_norm_w': 1.0 + nrm(ks[19], (DEPTH, GDN_DV), 0.05),
        'w_out': nrm(ks[20], (DEPTH, MIX_W, D_MODEL), MIX_W ** -0.5),
        'w_router': nrm(ks[21], (DEPTH, D_MODEL, N_EXPERTS), D_MODEL ** -0.5),
        'b_router': nrm(ks[22], (DEPTH, N_EXPERTS), 0.01),
        'w_up': nrm(ks[23], (DEPTH, N_EXPERTS, D_MODEL, 2 * D_FF), D_MODEL ** -0.5),
        'b_up': nrm(ks[24], (DEPTH, N_EXPERTS, 2 * D_FF), 0.01),
        'w_down': nrm(ks[25], (DEPTH, N_EXPERTS, D_FF, D_MODEL), D_FF ** -0.5),
        'b_down': nrm(ks[26], (DEPTH, N_EXPERTS, D_MODEL), 0.01),
    }


def reference(x_prompt, x_sample, cache_k, cache_v, state_conv, state_ssm, page_table, c_prompt, c_sample,
              w_ada, b_ada, norm_attn_w, norm_ffn_w, norm_final_w, w_in, rel_bias, conv_w, a_log, dt_bias,
              gdn_norm_w, w_out, w_router, b_router, w_up, b_up, w_down, b_down):
    xp, xs = x_prompt, x_sample
    kp_l, vp_l, cp_l, sp_l, ks_l, vs_l, cs_l, ss_l = [], [], [], [], [], [], [], []
    for l in range(DEPTH):
        lw = (w_ada[l], b_ada[l], norm_attn_w[l], norm_ffn_w[l], w_in[l], w_out[l], gdn_norm_w[l],
              w_router[l], b_router[l], w_up[l], b_up[l], w_down[l], b_down[l])

        def attn_p(q, k, v):
            return moba_prompt(q, k, v, rel_bias)

        def attn_s(q, k, v, l=l):
            return moba_sample(q, k, v, cache_k, cache_v, l, page_table, rel_bias)

        def gdn_p(qkv, b_raw, a_raw, l=l):
            return gdn_prompt(qkv, b_raw, a_raw, conv_w[l], a_log[l], dt_bias[l])

        def gdn_s(qkv, b_raw, a_raw, l=l):
            return gdn_sample(qkv, b_raw, a_raw, conv_w[l], a_log[l], dt_bias[l], state_conv[l], state_ssm[l])

        xp, kp, vp, cp, sp = layer_forward(xp, c_prompt, attn_p, gdn_p, *lw)
        xs, ks_, vs_, cs_, ss_ = layer_forward(xs, c_sample, attn_s, gdn_s, *lw)
        kp_l.append(kp)
        vp_l.append(vp)
        cp_l.append(cp)
        sp_l.append(sp)
        ks_l.append(ks_)
        vs_l.append(vs_)
        cs_l.append(cs_)
        ss_l.append(ss_)
    y_prompt = rms_norm(xp, norm_final_w)
    y_sample = rms_norm(xs, norm_final_w)
    return (y_prompt, y_sample, jnp.stack(kp_l), jnp.stack(vp_l), jnp.stack(cp_l), jnp.stack(sp_l),
            jnp.stack(ks_l), jnp.stack(vs_l), jnp.stack(cs_l), jnp.stack(ss_l))
```

```python
import functools
import math

import jax
import jax.numpy as jnp
from jax import lax
from jax.experimental import pallas as pl
from jax.experimental.pallas import tpu as pltpu

F32 = jnp.float32
BF16 = jnp.bfloat16
HIGHEST = lax.Precision.HIGHEST

ATT_HEAD_DIM = 64
ATT_GROUP = 2
MOBA_BLOCK = 256
MOBA_TOPK = 3
REL_BUCKETS = 32
REL_MAX_DIST = 4096
GDN_DK = 128
GDN_CONV = 4
GDN_CHUNK = 64
TOP_K = 4
SWIGLU_LIMIT = 7.0
SWIGLU_ALPHA = 1.702
EPS = 1e-6
NEG = -1e30

LANES = 128
SUBLANES = 8
VMEM_LIMIT = 48 * 1024 * 1024


def _cparams(sem):
    return pltpu.CompilerParams(dimension_semantics=sem, vmem_limit_bytes=VMEM_LIMIT)


def _dot(a, b, precision=None):
    return jnp.dot(a, b, precision=precision, preferred_element_type=F32)


def _dot_nt(a, b, precision=None):
    return lax.dot_general(a, b, (((1,), (1,)), ((), ())), precision=precision,
                           preferred_element_type=F32)


def _dot_tn(a, b, precision=None):
    return lax.dot_general(a, b, (((0,), (0,)), ((), ())), precision=precision,
                           preferred_element_type=F32)


def _silu(x):
    return x * jax.nn.sigmoid(x)


def _softplus(x):
    return jnp.maximum(x, 0.0) + jnp.log1p(jnp.exp(-jnp.abs(x)))


def _rel_bucket(dist):
    n = jnp.maximum(dist, 0)
    max_exact = REL_BUCKETS // 2
    nf = jnp.maximum(n, max_exact).astype(F32)
    large = max_exact + (jnp.log(nf / max_exact) / math.log(REL_MAX_DIST / max_exact)
                         * (REL_BUCKETS - max_exact)).astype(jnp.int32)
    return jnp.where(n < max_exact, n, jnp.minimum(large, REL_BUCKETS - 1))


def _ada_kernel(c_ref, w_ref, b_ref, o_ref):
    o_ref[...] = _dot(_silu(c_ref[...]), w_ref[...], HIGHEST) + b_ref[...]


def _ada(c, w_ada, b_ada):
    m, d = c.shape
    n = w_ada.shape[1]
    tn = d
    return pl.pallas_call(
        _ada_kernel,
        grid=(n // tn,),
        in_specs=[pl.BlockSpec((m, d), lambda j: (0, 0)),
                  pl.BlockSpec((d, tn), lambda j: (0, j)),
                  pl.BlockSpec((1, tn), lambda j: (0, j))],
        out_specs=pl.BlockSpec((m, tn), lambda j: (0, j)),
        out_shape=jax.ShapeDtypeStruct((m, n), F32),
        compiler_params=_cparams(("parallel",)),
        name="ada_modulation",
    )(c, w_ada, b_ada.reshape(1, n))


def _inproj_kernel(x_ref, sh_ref, sc_ref, nw_ref, w_ref, *out_refs, offs, heads_out, kv_heads):
    x = x_ref[0]
    xn = x * lax.rsqrt(jnp.mean(x * x, axis=-1, keepdims=True) + EPS) * nw_ref[...]
    h = (xn * (1.0 + sc_ref[0]) + sh_ref[0]).astype(BF16)
    o0, o1, o2, o3, o4, o5 = offs

    def proj(lo, hi):
        return _dot(h, w_ref[:, lo:hi])

    q_ref, k_ref, v_ref, conv_ref, z_ref, ba_ref = out_refs[:6]
    q_ref[0] = proj(o0, o1)
    k = proj(o1, o2)
    v = proj(o2, o3)
    k_ref[0] = k
    v_ref[0] = v
    conv_ref[0] = proj(o3, o4)
    z_ref[0] = proj(o4, o5)
    ba_ref[0] = proj(o5, o5 + LANES)
    if heads_out:
        kh_ref, vt_ref, km_ref = out_refs[6:]
        vt = v.T
        for b in range(x.shape[0] // MOBA_BLOCK):
            rows = slice(b * MOBA_BLOCK, (b + 1) * MOBA_BLOCK)
            kb = k[rows]
            km_ref[0, b] = jnp.mean(kb, axis=0, keepdims=True)
            for hh in range(kv_heads):
                cols = slice(hh * ATT_HEAD_DIM, (hh + 1) * ATT_HEAD_DIM)
                kh_ref[0, hh, b] = kb[:, cols].astype(BF16)
                vt_ref[0, hh, b] = vt[cols, rows].astype(BF16)


def _inproj(x, sh, sc, norm_w, w_pad, widths, tt, heads_out):
    n, t, d = x.shape
    per_token = sh.shape[1] != 1
    q_w, kv_w, conv_w_, z_w = widths
    offs = (0, q_w, q_w + kv_w, q_w + 2 * kv_w, q_w + 2 * kv_w + conv_w_, q_w + 2 * kv_w + conv_w_ + z_w)
    kv_heads = kv_w // ATT_HEAD_DIM
    nb_t = tt // MOBA_BLOCK
    mod_spec = (pl.BlockSpec((1, tt, d), lambda i, j: (i, j, 0)) if per_token
                else pl.BlockSpec((1, 1, d), lambda i, j: (i, 0, 0)))

    def tok(width):
        return pl.BlockSpec((1, tt, width), lambda i, j: (i, j, 0))

    out_specs = [tok(q_w), tok(kv_w), tok(kv_w), tok(conv_w_), tok(z_w), tok(LANES)]
    out_shape = [jax.ShapeDtypeStruct((n, t, w), F32) for w in (q_w, kv_w, kv_w, conv_w_, z_w, LANES)]
    if heads_out:
        nblk = t // MOBA_BLOCK
        out_specs += [
            pl.BlockSpec((1, kv_heads, nb_t, MOBA_BLOCK, ATT_HEAD_DIM), lambda i, j: (i, 0, j, 0, 0)),
            pl.BlockSpec((1, kv_heads, nb_t, ATT_HEAD_DIM, MOBA_BLOCK), lambda i, j: (i, 0, j, 0, 0)),
            pl.BlockSpec((1, nb_t, 1, kv_w), lambda i, j: (i, j, 0, 0)),
        ]
        out_shape += [
            jax.ShapeDtypeStruct((n, kv_heads, nblk, MOBA_BLOCK, ATT_HEAD_DIM), BF16),
            jax.ShapeDtypeStruct((n, kv_heads, nblk, ATT_HEAD_DIM, MOBA_BLOCK), BF16),
            jax.ShapeDtypeStruct((n, nblk, 1, kv_w), F32),
        ]
    return pl.pallas_call(
        functools.partial(_inproj_kernel, offs=offs, heads_out=heads_out, kv_heads=kv_heads),
        grid=(n, t // tt),
        in_specs=[pl.BlockSpec((1, tt, d), lambda i, j: (i, j, 0)), mod_spec, mod_spec,
                  pl.BlockSpec((1, d), lambda i, j: (0, 0)),
                  pl.BlockSpec(w_pad.shape, lambda i, j: (0, 0))],
        out_specs=out_specs,
        out_shape=out_shape,
        compiler_params=_cparams(("parallel", "parallel")),
        name="in_proj_prompt" if heads_out else "in_proj_sample",
    )(x, sh, sc, norm_w.reshape(1, d), w_pad)


def _bias_tile_kernel(rb_ref, o_ref):
    head = pl.program_id(0)
    delta = pl.program_id(1)
    key = lax.broadcasted_iota(jnp.int32, (MOBA_BLOCK, MOBA_BLOCK), 0)
    qry = lax.broadcasted_iota(jnp.int32, (MOBA_BLOCK, MOBA_BLOCK), 1)
    bucket = _rel_bucket(delta * MOBA_BLOCK + qry - key)
    acc = jnp.zeros((MOBA_BLOCK, MOBA_BLOCK), F32)
    for t in range(REL_BUCKETS):
        acc = jnp.where(bucket == t, rb_ref[t, head], acc)
    o_ref[0, 0] = acc


def _bias_tiles(rel_bias, nblk):
    heads = rel_bias.shape[1]
    return pl.pallas_call(
        _bias_tile_kernel,
        grid=(heads, nblk),
        in_specs=[pl.BlockSpec(memory_space=pltpu.SMEM)],
        out_specs=pl.BlockSpec((1, 1, MOBA_BLOCK, MOBA_BLOCK), lambda h, dlt: (h, dlt, 0, 0)),
        out_shape=jax.ShapeDtypeStruct((heads, nblk, MOBA_BLOCK, MOBA_BLOCK), F32),
        compiler_params=_cparams(("parallel", "parallel")),
        name="rel_bias_tiles",
    )(rel_bias)


def _bias_rows_kernel(tab_ref, o_ref, *, past, dec_seq):
    rows, width = o_ref.shape
    row = lax.broadcasted_iota(jnp.int32, (rows, width), 0)
    key = lax.broadcasted_iota(jnp.int32, (rows, width), 1)
    bucket = _rel_bucket(past + lax.rem(row, dec_seq) - key)
    acc = jnp.zeros((rows, width), F32)
    for t in range(REL_BUCKETS):
        acc = jnp.where(bucket == t, tab_ref[:, t:t + 1], acc)
    o_ref[...] = acc


def _bias_rows(rel_bias, past, dec_seq, width):
    heads = rel_bias.shape[1]
    rows = heads * dec_seq
    tab = jnp.repeat(rel_bias.T, dec_seq, axis=0)
    return pl.pallas_call(
        functools.partial(_bias_rows_kernel, past=past, dec_seq=dec_seq),
        out_shape=jax.ShapeDtypeStruct((rows, width), F32),
        name="rel_bias_rows",
    )(tab)


def _topk_rank_rows(gm, idx, n):
    rank = jnp.zeros(gm.shape, jnp.int32)
    for m in range(n):
        row = gm[m:m + 1, :]
        beats = (row > gm) | ((row == gm) & (m < idx))
        rank = rank + beats.astype(jnp.int32)
    return rank


def _attn_prompt_kernel(q_ref, kh_ref, vt_ref, km_ref, bias_ref, o_ref, sel_ref, *, nblk):
    qt = pl.program_id(2)
    q_t = q_ref[0].T
    km = km_ref[0, 0]
    tq = q_t.shape[1]
    blk = lax.broadcasted_iota(jnp.int32, (nblk, tq), 0)
    past = blk < qt
    key = lax.broadcasted_iota(jnp.int32, (MOBA_BLOCK, tq), 0)
    qry = lax.broadcasted_iota(jnp.int32, (MOBA_BLOCK, tq), 1)
    causal = key <= qry
    outs = []
    for g in range(ATT_GROUP):
        q_g = q_t[g * ATT_HEAD_DIM:(g + 1) * ATT_HEAD_DIM, :]
        gate = _dot(km, q_g, HIGHEST)
        gm = jnp.where(past, gate, NEG)
        rank = _topk_rank_rows(gm, blk, nblk)
        sel_ref[g] = (past & (rank < MOBA_TOPK)).astype(F32)
        qs = (q_g * (ATT_HEAD_DIM ** -0.5)).astype(BF16)

        def scores(kb, qs=qs):
            return _dot(kh_ref[0, 0, kb], qs)

        s = jnp.where(causal, scores(qt) + bias_ref[g, 0], NEG)
        m0 = jnp.max(s, axis=0, keepdims=True)
        p = jnp.exp(s - m0)
        l0 = jnp.sum(p, axis=0, keepdims=True)
        acc0 = _dot(vt_ref[0, 0, qt], p.astype(BF16))

        def body(kb, carry, g=g, scores=scores):
            m, l, acc = carry
            s = scores(kb) + bias_ref[g, qt - kb]
            s = jnp.where(sel_ref[g, pl.ds(kb, 1), :] > 0.5, s, NEG)
            m_new = jnp.maximum(m, jnp.max(s, axis=0, keepdims=True))
            alpha = jnp.exp(m - m_new)
            p = jnp.exp(s - m_new)
            l = alpha * l + jnp.sum(p, axis=0, keepdims=True)
            acc = alpha * acc + _dot(vt_ref[0, 0, kb], p.astype(BF16))
            return m_new, l, acc

        _, l, acc = lax.fori_loop(0, qt, body, (m0, l0, acc0))
        outs.append((acc / l).T)
    o_ref[0] = jnp.concatenate(outs, axis=1)


def _attn_prompt(q, kh, vt, km, bias):
    n, t, q_w = q.shape
    kv_heads, nblk = kh.shape[1], kh.shape[2]
    gw = ATT_GROUP * ATT_HEAD_DIM
    return pl.pallas_call(
        functools.partial(_attn_prompt_kernel, nblk=nblk),
        grid=(n, kv_heads, nblk),
        in_specs=[
            pl.BlockSpec((1, MOBA_BLOCK, gw), lambda i, h, j: (i, j, h)),
            pl.BlockSpec((1, 1, nblk, MOBA_BLOCK, ATT_HEAD_DIM), lambda i, h, j: (i, h, 0, 0, 0)),
            pl.BlockSpec((1, 1, nblk, ATT_HEAD_DIM, MOBA_BLOCK), lambda i, h, j: (i, h, 0, 0, 0)),
            pl.BlockSpec((1, 1, nblk, ATT_HEAD_DIM), lambda i, h, j: (i, h, 0, 0)),
            pl.BlockSpec((ATT_GROUP, nblk, MOBA_BLOCK, MOBA_BLOCK), lambda i, h, j: (h, 0, 0, 0)),
        ],
        out_specs=pl.BlockSpec((1, MOBA_BLOCK, gw), lambda i, h, j: (i, j, h)),
        out_shape=jax.ShapeDtypeStruct((n, t, q_w), F32),
        scratch_shapes=[pltpu.VMEM((ATT_GROUP, nblk, MOBA_BLOCK), F32)],
        compiler_params=_cparams(("parallel", "parallel", "arbitrary")),
        name="moba_prompt",
    )(q, kh, vt, km, bias)


def _attn_sample_kernel(pt_ref, qc_ref, knew_ref, vnew_ref, bias_ref, ck_hbm, cv_hbm, o_ref,
                        kbuf, vbuf, sem, *, n_pages, dec_seq, kv_heads):
    s = pl.program_id(0)
    nseq = pl.num_programs(0)
    slot = lax.rem(s, 2)
    pages_per_blk = MOBA_BLOCK // kbuf.shape[2]
    nblk = n_pages // pages_per_blk
    past_len = nblk * MOBA_BLOCK

    def start_fetch(seq, sl):
        def one(p, carry):
            pg = pt_ref[seq, p]
            pltpu.make_async_copy(ck_hbm.at[pg], kbuf.at[sl, p], sem.at[0, sl]).start()
            pltpu.make_async_copy(cv_hbm.at[pg], vbuf.at[sl, p], sem.at[1, sl]).start()
            return carry
        lax.fori_loop(0, n_pages, one, 0)

    @pl.when(s == 0)
    def _():
        start_fetch(0, 0)

    @pl.when(s + 1 < nseq)
    def _():
        start_fetch(s + 1, 1 - slot)

    pltpu.make_async_copy(ck_hbm.at[pl.ds(0, n_pages)], kbuf.at[slot], sem.at[0, slot]).wait()
    pltpu.make_async_copy(cv_hbm.at[pl.ds(0, n_pages)], vbuf.at[slot], sem.at[1, slot]).wait()

    qc = qc_ref[0]
    rows, kv_w = qc.shape

    def block(buf, b):
        return buf[slot, b * pages_per_blk:(b + 1) * pages_per_blk].reshape(MOBA_BLOCK, kv_w)

    kmean = jnp.concatenate(
        [jnp.mean(block(kbuf, b), axis=0, keepdims=True) for b in range(nblk)], axis=0)
    gate = _dot_nt(qc, kmean, HIGHEST)
    blk = lax.broadcasted_iota(jnp.int32, (rows, nblk), 1)
    rank = jnp.zeros((rows, nblk), jnp.int32)
    for m in range(nblk):
        col = gate[:, m:m + 1]
        rank = rank + ((col > gate) | ((col == gate) & (m < blk))).astype(jnp.int32)
    sel = rank < MOBA_TOPK

    qs = (qc * (ATT_HEAD_DIM ** -0.5)).astype(BF16)
    s_blocks = []
    for b in range(nblk):
        sb = _dot_nt(qs, block(kbuf, b).astype(BF16)) + bias_ref[:, b * MOBA_BLOCK:(b + 1) * MOBA_BLOCK]
        s_blocks.append(jnp.where(sel[:, b:b + 1], sb, NEG))
    n_new = knew_ref.shape[1]
    r_idx = lax.rem(lax.broadcasted_iota(jnp.int32, (rows, n_new), 0), dec_seq)
    j_idx = lax.broadcasted_iota(jnp.int32, (rows, n_new), 1)
    s_own = _dot_nt(qs, knew_ref[0].astype(BF16)) + bias_ref[:, past_len:past_len + n_new]
    s_own = jnp.where(j_idx <= r_idx, s_own, NEG)
    m = jnp.max(s_own, axis=1, keepdims=True)
    for sb in s_blocks:
        m = jnp.maximum(m, jnp.max(sb, axis=1, keepdims=True))
    p_own = jnp.exp(s_own - m)
    l = jnp.sum(p_own, axis=1, keepdims=True)
    acc = _dot(p_own.astype(BF16), vnew_ref[0].astype(BF16))
    for b, sb in enumerate(s_blocks):
        p = jnp.exp(sb - m)
        l = l + jnp.sum(p, axis=1, keepdims=True)
        acc = acc + _dot(p.astype(BF16), block(vbuf, b).astype(BF16))
    out = acc / l
    rpk = rows // kv_heads
    for h in range(kv_heads):
        o_ref[0, h * rpk:(h + 1) * rpk, :] = out[h * rpk:(h + 1) * rpk,
                                                 h * ATT_HEAD_DIM:(h + 1) * ATT_HEAD_DIM]


def _attn_sample(page_table, qc, k_new, v_new, bias, cache_k, cache_v, dec_seq, kv_heads):
    b, rows, kv_w = qc.shape
    n_pages = page_table.shape[1]
    page = cache_k.shape[1]
    n_new = k_new.shape[1]
    grid_spec = pltpu.PrefetchScalarGridSpec(
        num_scalar_prefetch=1,
        grid=(b,),
        in_specs=[
            pl.BlockSpec((1, rows, kv_w), lambda i, pt: (i, 0, 0)),
            pl.BlockSpec((1, n_new, kv_w), lambda i, pt: (i, 0, 0)),
            pl.BlockSpec((1, n_new, kv_w), lambda i, pt: (i, 0, 0)),
            pl.BlockSpec(bias.shape, lambda i, pt: (0, 0)),
            pl.BlockSpec(memory_space=pl.ANY),
            pl.BlockSpec(memory_space=pl.ANY),
        ],
        out_specs=pl.BlockSpec((1, rows, ATT_HEAD_DIM), lambda i, pt: (i, 0, 0)),
        scratch_shapes=[pltpu.VMEM((2, n_pages, page, kv_w), F32),
                        pltpu.VMEM((2, n_pages, page, kv_w), F32),
                        pltpu.SemaphoreType.DMA((2, 2))],
    )
    return pl.pallas_call(
        functools.partial(_attn_sample_kernel, n_pages=n_pages, dec_seq=dec_seq, kv_heads=kv_heads),
        grid_spec=grid_spec,
        out_shape=jax.ShapeDtypeStruct((b, rows, ATT_HEAD_DIM), F32),
        compiler_params=_cparams(("arbitrary",)),
        name="moba_sample",
    )(page_table, qc, k_new, v_new, bias, cache_k, cache_v)


def _gdn_gates(ba, alog, dt):
    return jax.nn.sigmoid(ba), -jnp.exp(alog) * _softplus(ba + dt)


def _gdn_prompt_kernel(x_ref, ba_ref, bat_ref, cw_ref, alog_ref, dt_ref, alogt_ref, dtt_ref,
                       o_ref, s_ref, ext_ref, *, heads):
    ct, cw = x_ref.shape[1], x_ref.shape[2]
    qk_w = heads * GDN_DK
    cs = GDN_CHUNK
    halo = SUBLANES

    @pl.when(pl.program_id(1) == 0)
    def _():
        ext_ref[0:halo, :] = jnp.zeros((halo, cw), F32)
        s_ref[...] = jnp.zeros(s_ref.shape, F32)

    x = x_ref[0]
    ext_ref[halo:halo + ct, :] = x
    w = cw_ref[...]
    conv = x * w[GDN_CONV - 1:GDN_CONV, :]
    for i in range(GDN_CONV - 1):
        conv = conv + ext_ref[pl.ds(halo - (GDN_CONV - 1) + i, ct), :] * w[i:i + 1, :]
    ext_ref[0:halo, :] = x[ct - halo:ct, :]
    act = _silu(conv)

    beta_f, g_f = _gdn_gates(ba_ref[0], alog_ref[...], dt_ref[...])
    _, g_t = _gdn_gates(bat_ref[0], alogt_ref[...], dtt_ref[...])

    r_i = lax.broadcasted_iota(jnp.int32, (cs, cs), 0)
    c_i = lax.broadcasted_iota(jnp.int32, (cs, cs), 1)
    tril = r_i >= c_i
    strict = r_i > c_i
    eye = (r_i == c_i).astype(F32)
    cum_l = tril.astype(F32)
    cum_u = (r_i <= c_i).astype(F32)

    qn, kn, vv = [], [], []
    for h in range(heads):
        qh = act[:, h * GDN_DK:(h + 1) * GDN_DK]
        kh = act[:, qk_w + h * GDN_DK:qk_w + (h + 1) * GDN_DK]
        qn.append(qh * lax.rsqrt(jnp.sum(qh * qh, axis=-1, keepdims=True) + EPS) * (GDN_DK ** -0.5))
        kn.append(kh * lax.rsqrt(jnp.sum(kh * kh, axis=-1, keepdims=True) + EPS))
        vv.append(act[:, 2 * qk_w + h * GDN_DK:2 * qk_w + (h + 1) * GDN_DK])

    state = [s_ref[0, h] for h in range(heads)]
    for c in range(ct // cs):
        rows = slice(c * cs, (c + 1) * cs)
        gc = _dot(cum_l, g_f[rows, :], HIGHEST)
        gct = _dot(g_t[:, rows], cum_u, HIGHEST)
        for h in range(heads):
            q_c, k_c, v_c = qn[h][rows], kn[h][rows], vv[h][rows]
            beta = beta_f[rows, h:h + 1]
            gcc = gc[:, heads + h:heads + h + 1]
            gcr = gct[heads + h:heads + h + 1, :]
            decay = jnp.where(tril, jnp.exp(jnp.where(tril, gcc - gcr, 0.0)), 0.0)
            kbeta = k_c * beta
            egc = jnp.exp(gcc)
            a = _dot_nt(jnp.concatenate([kbeta, q_c], axis=0), k_c, HIGHEST)
            x1 = -jnp.where(strict, a[:cs] * decay, 0.0)
            intra = jnp.where(tril, a[cs:] * decay, 0.0)
            p = eye + x1
            xp = _dot(x1, x1, HIGHEST)
            npow = 2
            while npow * 2 < cs:
                pr = _dot(jnp.concatenate([p, xp], axis=0), xp, HIGHEST)
                p = p + pr[:cs]
                xp = pr[cs:]
                npow *= 2
            p = p + _dot(p, xp, HIGHEST)
            uw = _dot(p, jnp.concatenate([v_c * beta, kbeta * egc], axis=1), HIGHEST)
            u, wmat = uw[:, :GDN_DK], uw[:, GDN_DK:]
            st = state[h]
            ws = _dot(jnp.concatenate([wmat, q_c * egc], axis=0), st, HIGHEST)
            v_new = u - ws[:cs]
            o_ref[0, rows, h * GDN_DK:(h + 1) * GDN_DK] = ws[cs:] + _dot(intra, v_new, HIGHEST)
            g_last = gcc[cs - 1:cs, :]
            state[h] = st * jnp.exp(g_last) + _dot_tn(k_c * jnp.exp(g_last - gcc), v_new, HIGHEST)
    for h in range(heads):
        s_ref[0, h] = state[h]


def _gdn_prompt(conv_pre, ba, conv_w, a_log, dt_bias, heads, ct):
    n, t, cw = conv_pre.shape
    bat = jnp.swapaxes(ba[:, :, :2 * heads], 1, 2)
    pad = jnp.zeros((heads,), F32)
    alog = jnp.concatenate([pad, a_log])
    dt = jnp.concatenate([pad, dt_bias])
    lane_pad = (0, LANES - 2 * heads)
    return pl.pallas_call(
        functools.partial(_gdn_prompt_kernel, heads=heads),
        grid=(n, t // ct),
        in_specs=[
            pl.BlockSpec((1, ct, cw), lambda i, j: (i, j, 0)),
            pl.BlockSpec((1, ct, LANES), lambda i, j: (i, j, 0)),
            pl.BlockSpec((1, 2 * heads, ct), lambda i, j: (i, 0, j)),
            pl.BlockSpec(conv_w.shape, lambda i, j: (0, 0)),
            pl.BlockSpec((1, LANES), lambda i, j: (0, 0)),
            pl.BlockSpec((1, LANES), lambda i, j: (0, 0)),
            pl.BlockSpec((2 * heads, 1), lambda i, j: (0, 0)),
            pl.BlockSpec((2 * heads, 1), lambda i, j: (0, 0)),
        ],
        out_specs=[pl.BlockSpec((1, ct, heads * GDN_DK), lambda i, j: (i, j, 0)),
                   pl.BlockSpec((1, heads, GDN_DK, GDN_DK), lambda i, j: (i, 0, 0, 0))],
        out_shape=[jax.ShapeDtypeStruct((n, t, heads * GDN_DK), F32),
                   jax.ShapeDtypeStruct((n, heads, GDN_DK, GDN_DK), F32)],
        scratch_shapes=[pltpu.VMEM((SUBLANES + ct, cw), F32)],
        compiler_params=_cparams(("parallel", "arbitrary")),
        name="gdn_prompt",
    )(conv_pre, ba, bat, conv_w, jnp.pad(alog, lane_pad).reshape(1, LANES),
      jnp.pad(dt, lane_pad).reshape(1, LANES), alog.reshape(2 * heads, 1), dt.reshape(2 * heads, 1))


def _gdn_sample_kernel(ext_ref, ba_ref, cw_ref, alog_ref, dt_ref, s_in_ref, o_ref, s_out_ref, *, heads):
    t_len = ba_ref.shape[0]
    gs = ext_ref.shape[1]
    qk_w = heads * GDN_DK
    w = cw_ref[...]
    q_t, k_t, v_r, beta_r, dec_r = [], [], [], [], []
    for t in range(t_len):
        conv = ext_ref[t] * w[0:1, :]
        for i in range(1, GDN_CONV):
            conv = conv + ext_ref[t + i] * w[i:i + 1, :]
        act = _silu(conv)
        beta_f, g_f = _gdn_gates(ba_ref[t], alog_ref[...], dt_ref[...])
        beta_r.append(beta_f)
        dec_r.append(jnp.exp(g_f))
        qs, ks, vs = [], [], []
        for h in range(heads):
            qh = act[:, h * GDN_DK:(h + 1) * GDN_DK]
            kh = act[:, qk_w + h * GDN_DK:qk_w + (h + 1) * GDN_DK]
            qn = qh * lax.rsqrt(jnp.sum(qh * qh, axis=-1, keepdims=True) + EPS) * (GDN_DK ** -0.5)
            kn = kh * lax.rsqrt(jnp.sum(kh * kh, axis=-1, keepdims=True) + EPS)
            qs.append(qn.T)
            ks.append(kn.T)
            vs.append(act[:, 2 * qk_w + h * GDN_DK:2 * qk_w + (h + 1) * GDN_DK])
        q_t.append(qs)
        k_t.append(ks)
        v_r.append(vs)
    for i in range(gs):
        for h in range(heads):
            st = s_in_ref[i, h]
            for t in range(t_len):
                st = st * dec_r[t][i:i + 1, heads + h:heads + h + 1]
                kcol = k_t[t][h][:, i:i + 1]
                ks_row = jnp.sum(st * kcol, axis=0, keepdims=True)
                delta = (v_r[t][h][i:i + 1, :] - ks_row) * beta_r[t][i:i + 1, h:h + 1]
                st = st + kcol * delta
                o_ref[t, i:i + 1, h * GDN_DK:(h + 1) * GDN_DK] = jnp.sum(
                    st * q_t[t][h][:, i:i + 1], axis=0, keepdims=True)
            s_out_ref[i, h] = st


def _gdn_sample(ext_tm, ba_tm, conv_w, a_log, dt_bias, state, heads, gs):
    t_len, b, _ = ba_tm.shape
    cw = ext_tm.shape[2]
    pad = jnp.zeros((heads,), F32)
    lane_pad = (0, LANES - 2 * heads)
    alog = jnp.pad(jnp.concatenate([pad, a_log]), lane_pad).reshape(1, LANES)
    dt = jnp.pad(jnp.concatenate([pad, dt_bias]), lane_pad).reshape(1, LANES)
    return pl.pallas_call(
        functools.partial(_gdn_sample_kernel, heads=heads),
        grid=(b // gs,),
        in_specs=[
            pl.BlockSpec((ext_tm.shape[0], gs, cw), lambda i: (0, i, 0)),
            pl.BlockSpec((t_len, gs, LANES), lambda i: (0, i, 0)),
            pl.BlockSpec(conv_w.shape, lambda i: (0, 0)),
            pl.BlockSpec((1, LANES), lambda i: (0, 0)),
            pl.BlockSpec((1, LANES), lambda i: (0, 0)),
            pl.BlockSpec((gs, heads, GDN_DK, GDN_DK), lambda i: (i, 0, 0, 0)),
        ],
        out_specs=[pl.BlockSpec((t_len, gs, heads * GDN_DK), lambda i: (0, i, 0)),
                   pl.BlockSpec((gs, heads, GDN_DK, GDN_DK), lambda i: (i, 0, 0, 0))],
        out_shape=[jax.ShapeDtypeStruct((t_len, b, heads * GDN_DK), F32),
                   jax.ShapeDtypeStruct(state.shape, F32)],
        compiler_params=_cparams(("parallel",)),
        name="gdn_sample",
    )(ext_tm, ba_tm, conv_w, alog, dt, state)


def _mid_kernel(x_ref, attn_ref, og_ref, z_ref, gta_ref, scf_ref, shf_ref, gw_ref, wout_ref, nf_ref,
                wr_ref, br_ref, x1_ref, h2_ref, gates_ref, *, heads):
    og = og_ref[0]
    z = z_ref[0]
    parts = [attn_ref[0]]
    for h in range(heads):
        o = og[:, h * GDN_DK:(h + 1) * GDN_DK]
        y = o * lax.rsqrt(jnp.mean(o * o, axis=-1, keepdims=True) + EPS) * gw_ref[...]
        parts.append(y * _silu(z[:, h * GDN_DK:(h + 1) * GDN_DK]))
    mix = jnp.concatenate(parts, axis=1).astype(BF16)
    x1 = x_ref[0] + gta_ref[0] * _dot(mix, wout_ref[...])
    x1_ref[0] = x1
    h2 = (x1 * lax.rsqrt(jnp.mean(x1 * x1, axis=-1, keepdims=True) + EPS) * nf_ref[...]
          * (1.0 + scf_ref[0]) + shf_ref[0])
    h2_ref[0] = h2.astype(BF16)
    logits = _dot(h2, wr_ref[...], HIGHEST) + br_ref[...]
    lane = lax.broadcasted_iota(jnp.int32, logits.shape, 1)
    work = logits
    sel = jnp.zeros(logits.shape, jnp.bool_)
    top = None
    for j in range(TOP_K):
        mx = jnp.max(work, axis=-1, keepdims=True)
        idx = jnp.min(jnp.where(work == mx, lane, LANES), axis=-1, keepdims=True)
        pick = lane == idx
        if j == 0:
            top = mx
        sel = sel | pick
        work = jnp.where(pick, -jnp.inf, work)
    e = jnp.where(sel, jnp.exp(logits - top), 0.0)
    gates_ref[0] = e / jnp.sum(e, axis=-1, keepdims=True)


def _mid(x, attn, og, z, gta, scf, shf, gdn_norm_w, w_out_bf, norm_ffn_w, wr_pad, br_pad, tt, heads):
    n, t, d = x.shape
    per_token = gta.shape[1] != 1
    mod_spec = (pl.BlockSpec((1, tt, d), lambda i, j: (i, j, 0)) if per_token
                else pl.BlockSpec((1, 1, d), lambda i, j: (i, 0, 0)))

    def tok(width):
        return pl.BlockSpec((1, tt, width), lambda i, j: (i, j, 0))

    def full(a):
        return pl.BlockSpec(a.shape, lambda i, j: (0,) * a.ndim)

    gw = gdn_norm_w.reshape(1, -1)
    nf = norm_ffn_w.reshape(1, d)
    return pl.pallas_call(
        functools.partial(_mid_kernel, heads=heads),
        grid=(n, t // tt),
        in_specs=[tok(d), tok(attn.shape[2]), tok(og.shape[2]), tok(z.shape[2]), mod_spec, mod_spec, mod_spec,
                  full(gw), full(w_out_bf), full(nf), full(wr_pad), full(br_pad)],
        out_specs=[tok(d), tok(d), tok(LANES)],
        out_shape=[jax.ShapeDtypeStruct((n, t, d), F32), jax.ShapeDtypeStruct((n, t, d), BF16),
                   jax.ShapeDtypeStruct((n, t, LANES), F32)],
        compiler_params=_cparams(("parallel", "parallel")),
        name="out_proj_router",
    )(x, attn, og, z, gta, scf, shf, gw, w_out_bf, nf, wr_pad, br_pad)


def _moe_kernel(h2_ref, gates_ref, x1_ref, gtf_ref, wup_ref, bup_ref, wdn_ref, bdn_ref, nw_ref,
                y_ref, acc_ref, *, ff_chunk):
    e = pl.program_id(2)
    n_exp = pl.num_programs(2)

    @pl.when(e == 0)
    def _():
        acc_ref[...] = jnp.zeros(acc_ref.shape, F32)

    h = h2_ref[0]
    gates = gates_ref[0]
    lane = lax.broadcasted_iota(jnp.int32, gates.shape, 1)
    gcol = jnp.sum(jnp.where(lane == e, gates, 0.0), axis=-1, keepdims=True)
    d_ff = wdn_ref.shape[1]
    y = jnp.zeros(acc_ref.shape, F32)
    for c in range(d_ff // ff_chunk):
        lo, hi = c * ff_chunk, (c + 1) * ff_chunk
        x_glu = _dot(h, wup_ref[0, :, lo:hi]) + bup_ref[0, :, lo:hi]
        x_lin = _dot(h, wup_ref[0, :, d_ff + lo:d_ff + hi]) + bup_ref[0, :, d_ff + lo:d_ff + hi]
        x_glu = jnp.minimum(x_glu, SWIGLU_LIMIT)
        x_lin = jnp.clip(x_lin, -SWIGLU_LIMIT, SWIGLU_LIMIT)
        act = x_glu * jax.nn.sigmoid(SWIGLU_ALPHA * x_glu) * (x_lin + 1.0)
        y = y + _dot(act.astype(BF16), wdn_ref[0, lo:hi, :])
    acc_ref[...] += gcol * (y + bdn_ref[0])

    @pl.when(e == n_exp - 1)
    def _():
        x = x1_ref[0] + gtf_ref[0] * acc_ref[...]
        y_ref[0] = x * lax.rsqrt(jnp.mean(x * x, axis=-1, keepdims=True) + EPS) * nw_ref[...]


def _moe(h2, gates, x1, gtf, w_up_bf, b_up, w_dn_bf, b_dn, norm_final_w, tm):
    n, t, d = x1.shape
    n_exp, _, ff2 = w_up_bf.shape
    d_ff = ff2 // 2
    per_token = gtf.shape[1] != 1
    mod_spec = (pl.BlockSpec((1, tm, d), lambda i, j, e: (i, j, 0)) if per_token
                else pl.BlockSpec((1, 1, d), lambda i, j, e: (i, 0, 0)))

    def tok(width):
        return pl.BlockSpec((1, tm, width), lambda i, j, e: (i, j, 0))

    return pl.pallas_call(
        functools.partial(_moe_kernel, ff_chunk=min(512, d_ff)),
        grid=(n, t // tm, n_exp),
        in_specs=[tok(d), tok(LANES), tok(d), mod_spec,
                  pl.BlockSpec((1, d, ff2), lambda i, j, e: (e, 0, 0)),
                  pl.BlockSpec((1, 1, ff2), lambda i, j, e: (e, 0, 0)),
                  pl.BlockSpec((1, d_ff, d), lambda i, j, e: (e, 0, 0)),
                  pl.BlockSpec((1, 1, d), lambda i, j, e: (e, 0, 0)),
                  pl.BlockSpec((1, d), lambda i, j, e: (0, 0))],
        out_specs=tok(d),
        out_shape=jax.ShapeDtypeStruct((n, t, d), F32),
        scratch_shapes=[pltpu.VMEM((tm, d), F32)],
        compiler_params=_cparams(("parallel", "parallel", "arbitrary")),
        name="moe_ffn",
    )(h2, gates, x1, gtf, w_up_bf, b_up.reshape(n_exp, 1, ff2), w_dn_bf, b_dn.reshape(n_exp, 1, d),
      norm_final_w.reshape(1, d))


def _pick_tile(t, pref):
    tile = min(t, pref)
    assert t % tile == 0, (t, tile)
    return tile


def kernel(x_prompt, x_sample, cache_k, cache_v, state_conv, state_ssm, page_table, c_prompt, c_sample, w_ada, b_ada, norm_attn_w, norm_ffn_w, norm_final_w, w_in, rel_bias, conv_w, a_log, dt_bias, gdn_norm_w, w_out, w_router, b_router, w_up, b_up, w_down, b_down):
    depth = w_in.shape[0]
    assert depth == 1, "single-layer step"
    nb, seq, d = x_prompt.shape
    db, dec_seq, _ = x_sample.shape
    n_pool, page, kv_heads, dh = cache_k.shape[1:]
    assert dh == ATT_HEAD_DIM
    att_heads = rel_bias.shape[1]
    assert att_heads == kv_heads * ATT_GROUP
    gdn_heads = a_log.shape[1]
    n_pages = page_table.shape[1]
    past = n_pages * page
    q_w = att_heads * dh
    kv_w = kv_heads * dh
    gdn_cw = conv_w.shape[2]
    z_w = gdn_heads * GDN_DK
    assert gdn_cw == 3 * z_w
    in_w = q_w + 2 * kv_w + gdn_cw + z_w + 2 * gdn_heads
    assert w_in.shape[2] == in_w and in_w - 2 * gdn_heads == (in_w // LANES) * LANES
    assert seq % MOBA_BLOCK == 0 and past % MOBA_BLOCK == 0 and MOBA_BLOCK % page == 0
    assert dec_seq <= MOBA_BLOCK and dec_seq >= GDN_CONV - 1 and dec_seq <= SUBLANES
    n_exp = w_router.shape[2]
    nblk = seq // MOBA_BLOCK
    widths = (q_w, kv_w, gdn_cw, z_w)

    n_mod = nb + db
    n_mod_pad = -(-n_mod // SUBLANES) * SUBLANES
    c_all = jnp.pad(jnp.concatenate([c_prompt, c_sample], axis=0), ((0, n_mod_pad - n_mod), (0, 0)))
    mod = _ada(c_all, w_ada[0], b_ada[0])
    mod_p = [mod[:nb, i * d:(i + 1) * d].reshape(nb, 1, d) for i in range(6)]
    m_s = db * dec_seq
    mod_s = [jnp.broadcast_to(mod[nb:n_mod, None, i * d:(i + 1) * d], (db, dec_seq, d)).reshape(1, m_s, d)
             for i in range(6)]

    w_in_pad = jnp.pad(w_in[0], ((0, 0), (0, LANES - 2 * gdn_heads))).astype(BF16)
    w_out_bf = w_out[0].astype(BF16)
    wr_pad = jnp.pad(w_router[0], ((0, 0), (0, LANES - n_exp)))
    br_pad = jnp.pad(b_router[0], (0, LANES - n_exp), constant_values=NEG).reshape(1, LANES)
    w_up_bf = w_up[0].astype(BF16)
    w_dn_bf = w_down[0].astype(BF16)

    tt_p = _pick_tile(seq, 512)
    q_p, k_p, v_p, conv_p, z_p, ba_p, kh_p, vt_p, km_p = _inproj(
        x_prompt, mod_p[0], mod_p[1], norm_attn_w[0], w_in_pad, widths, tt_p, True)
    km_h = km_p.reshape(nb, nblk, kv_heads, dh).transpose(0, 2, 1, 3)
    bias_t = _bias_tiles(rel_bias, nblk)
    attn_p = _attn_prompt(q_p, kh_p, vt_p, km_h, bias_t)
    og_p, ssm_p = _gdn_prompt(conv_p, ba_p, conv_w[0], a_log[0], dt_bias[0], gdn_heads, _pick_tile(seq, 256))
    x1_p, h2_p, gates_p = _mid(x_prompt, attn_p, og_p, z_p, mod_p[2], mod_p[4], mod_p[3], gdn_norm_w[0],
                               w_out_bf, norm_ffn_w[0], wr_pad, br_pad, tt_p, gdn_heads)
    y_p = _moe(h2_p, gates_p, x1_p, mod_p[5], w_up_bf, b_up[0], w_dn_bf, b_down[0], norm_final_w,
               _pick_tile(seq, 512))

    tt_s = _pick_tile(m_s, 512)
    q_s, k_s, v_s, conv_s, z_s, ba_s = _inproj(
        x_sample.reshape(1, m_s, d), mod_s[0], mod_s[1], norm_attn_w[0], w_in_pad, widths, tt_s, False)
    q4 = q_s.reshape(db, dec_seq, kv_heads, ATT_GROUP, dh).transpose(0, 2, 3, 1, 4)
    q4 = q4.reshape(db, kv_heads, ATT_GROUP * dec_seq, 1, dh)
    eye = jnp.eye(kv_heads, dtype=F32).reshape(1, kv_heads, 1, kv_heads, 1)
    qc = (q4 * eye).reshape(db, att_heads * dec_seq, kv_w)
    new_pad = ((0, 0), (0, SUBLANES - dec_seq), (0, 0))
    k_new = jnp.pad(k_s.reshape(db, dec_seq, kv_w), new_pad)
    v_new = jnp.pad(v_s.reshape(db, dec_seq, kv_w), new_pad)
    bias_r = _bias_rows(rel_bias, past, dec_seq, past + MOBA_BLOCK)
    attn_rows = _attn_sample(page_table, qc, k_new, v_new, bias_r,
                             cache_k[0].reshape(n_pool, page, kv_w), cache_v[0].reshape(n_pool, page, kv_w),
                             dec_seq, kv_heads)
    attn_s = attn_rows.reshape(db, att_heads, dec_seq, dh).transpose(0, 2, 1, 3).reshape(1, m_s, q_w)

    conv_s3 = conv_s.reshape(db, dec_seq, gdn_cw)
    ext = jnp.concatenate([state_conv[0], conv_s3], axis=1)
    gs = _pick_tile(db, SUBLANES)
    og_tm, ssm_s = _gdn_sample(jnp.swapaxes(ext, 0, 1), jnp.swapaxes(ba_s.reshape(db, dec_seq, LANES), 0, 1),
                               conv_w[0], a_log[0], dt_bias[0], state_ssm[0], gdn_heads, gs)
    og_s = jnp.swapaxes(og_tm, 0, 1).reshape(1, m_s, z_w)
    x1_s, h2_s, gates_s = _mid(x_sample.reshape(1, m_s, d), attn_s, og_s, z_s, mod_s[2], mod_s[4], mod_s[3],
                               gdn_norm_w[0], w_out_bf, norm_ffn_w[0], wr_pad, br_pad, tt_s, gdn_heads)
    y_s = _moe(h2_s, gates_s, x1_s, mod_s[5], w_up_bf, b_up[0], w_dn_bf, b_down[0], norm_final_w, tt_s)

    return (
        y_p,
        y_s.reshape(db, dec_seq, d),
        k_p.reshape(1, nb, seq, kv_heads, dh),
        v_p.reshape(1, nb, seq, kv_heads, dh),
        conv_p[:, seq - (GDN_CONV - 1):, :][None],
        ssm_p[None],
        k_s.reshape(1, db, dec_seq, kv_heads, dh),
        v_s.reshape(1, db, dec_seq, kv_heads, dh),
        ext[:, dec_seq:, :][None],
        ssm_s[None],
    )
```

```python
import functools
import math

import jax
import jax.numpy as jnp
from jax import lax
from jax.experimental import pallas as pl
from jax.experimental.pallas import tpu as pltpu

F32 = jnp.float32
BF16 = jnp.bfloat16
HIGHEST = lax.Precision.HIGHEST

ATT_HEAD_DIM = 64
ATT_GROUP = 2
MOBA_BLOCK = 256
MOBA_TOPK = 3
REL_BUCKETS = 32
REL_MAX_DIST = 4096
GDN_DK = 128
GDN_CONV = 4
GDN_CHUNK = 64
TOP_K = 4
SWIGLU_LIMIT = 7.0
SWIGLU_ALPHA = 1.702
EPS = 1e-6
NEG = -1e30

LANES = 128
SUBLANES = 8
VMEM_LIMIT = 48 * 1024 * 1024
MOE_VMEM_LIMIT = 56 * 1024 * 1024


def _cparams(sem):
    return pltpu.CompilerParams(dimension_semantics=sem, vmem_limit_bytes=VMEM_LIMIT)


def _dot(a, b, precision=None):
    return jnp.dot(a, b, precision=precision, preferred_element_type=F32)


def _dot_nt(a, b, precision=None):
    return lax.dot_general(a, b, (((1,), (1,)), ((), ())), precision=precision,
                           preferred_element_type=F32)


def _dot_tn(a, b, precision=None):
    return lax.dot_general(a, b, (((0,), (0,)), ((), ())), precision=precision,
                           preferred_element_type=F32)


def _split(x):
    hi = x.astype(BF16)
    return hi, (x - hi.astype(F32)).astype(BF16)


def _mm3(a, b):
    m = a.shape[0]
    ah, al = _split(a)
    bh, bl = _split(b)
    t = _dot(jnp.concatenate([ah, al], axis=0), bh)
    return t[:m] + t[m:] + _dot(ah, bl)


def _mm3_nt(a, b):
    m = a.shape[0]
    ah, al = _split(a)
    bh, bl = _split(b)
    t = _dot_nt(jnp.concatenate([ah, al], axis=0), bh)
    return t[:m] + t[m:] + _dot_nt(ah, bl)


def _mm3_tn(a, b):
    n = b.shape[1]
    ah, al = _split(a)
    bh, bl = _split(b)
    t = _dot_tn(ah, jnp.concatenate([bh, bl], axis=1))
    return t[:, :n] + t[:, n:] + _dot_tn(al, bh)


def _silu(x):
    return x * jax.nn.sigmoid(x)


def _softplus(x):
    return jnp.maximum(x, 0.0) + jnp.log1p(jnp.exp(-jnp.abs(x)))


def _rel_bucket(dist):
    n = jnp.maximum(dist, 0)
    max_exact = REL_BUCKETS // 2
    nf = jnp.maximum(n, max_exact).astype(F32)
    large = max_exact + (jnp.log(nf / max_exact) / math.log(REL_MAX_DIST / max_exact)
                         * (REL_BUCKETS - max_exact)).astype(jnp.int32)
    return jnp.where(n < max_exact, n, jnp.minimum(large, REL_BUCKETS - 1))


def _ada_kernel(c_ref, w_ref, b_ref, o_ref):
    o_ref[...] = _dot(_silu(c_ref[...]), w_ref[...], HIGHEST) + b_ref[...]


def _ada(c, w_ada, b_ada):
    m, d = c.shape
    n = w_ada.shape[1]
    tn = d
    return pl.pallas_call(
        _ada_kernel,
        grid=(n // tn,),
        in_specs=[pl.BlockSpec((m, d), lambda j: (0, 0)),
                  pl.BlockSpec((d, tn), lambda j: (0, j)),
                  pl.BlockSpec((1, tn), lambda j: (0, j))],
        out_specs=pl.BlockSpec((m, tn), lambda j: (0, j)),
        out_shape=jax.ShapeDtypeStruct((m, n), F32),
        compiler_params=_cparams(("parallel",)),
        name="ada_modulation",
    )(c, w_ada, b_ada.reshape(1, n))


def _inproj_kernel(x_ref, sh_ref, sc_ref, nw_ref, w_ref, *out_refs, offs, heads_out, kv_heads):
    x = x_ref[0]
    xn = x * lax.rsqrt(jnp.mean(x * x, axis=-1, keepdims=True) + EPS) * nw_ref[...]
    h = (xn * (1.0 + sc_ref[0]) + sh_ref[0]).astype(BF16)
    o0, o1, o2, o3, o4, o5 = offs

    def proj(lo, hi):
        return _dot(h, w_ref[:, lo:hi])

    q_ref, k_ref, v_ref, conv_ref, z_ref, ba_ref = out_refs[:6]
    q_ref[0] = proj(o0, o1)
    k = proj(o1, o2)
    v = proj(o2, o3)
    k_ref[0] = k
    v_ref[0] = v
    conv_ref[0] = proj(o3, o4)
    z_ref[0] = proj(o4, o5)
    ba_ref[0] = proj(o5, o5 + LANES)
    if heads_out:
        kh_ref, vt_ref, km_ref = out_refs[6:]
        vt = v.T
        for b in range(x.shape[0] // MOBA_BLOCK):
            rows = slice(b * MOBA_BLOCK, (b + 1) * MOBA_BLOCK)
            kb = k[rows]
            km_ref[0, b] = jnp.mean(kb, axis=0, keepdims=True)
            for hh in range(kv_heads):
                cols = slice(hh * ATT_HEAD_DIM, (hh + 1) * ATT_HEAD_DIM)
                kh_ref[0, hh, b] = kb[:, cols].astype(BF16)
                vt_ref[0, hh, b] = vt[cols, rows].astype(BF16)


def _inproj(x, sh, sc, norm_w, w_pad, widths, tt, heads_out):
    n, t, d = x.shape
    per_token = sh.shape[1] != 1
    q_w, kv_w, conv_w_, z_w = widths
    offs = (0, q_w, q_w + kv_w, q_w + 2 * kv_w, q_w + 2 * kv_w + conv_w_, q_w + 2 * kv_w + conv_w_ + z_w)
    kv_heads = kv_w // ATT_HEAD_DIM
    nb_t = tt // MOBA_BLOCK
    mod_spec = (pl.BlockSpec((1, tt, d), lambda i, j: (i, j, 0)) if per_token
                else pl.BlockSpec((1, 1, d), lambda i, j: (i, 0, 0)))

    def tok(width):
        return pl.BlockSpec((1, tt, width), lambda i, j: (i, j, 0))

    out_specs = [tok(q_w), tok(kv_w), tok(kv_w), tok(conv_w_), tok(z_w), tok(LANES)]
    out_shape = [jax.ShapeDtypeStruct((n, t, w), F32) for w in (q_w, kv_w, kv_w, conv_w_, z_w, LANES)]
    if heads_out:
        nblk = t // MOBA_BLOCK
        out_specs += [
            pl.BlockSpec((1, kv_heads, nb_t, MOBA_BLOCK, ATT_HEAD_DIM), lambda i, j: (i, 0, j, 0, 0)),
            pl.BlockSpec((1, kv_heads, nb_t, ATT_HEAD_DIM, MOBA_BLOCK), lambda i, j: (i, 0, j, 0, 0)),
            pl.BlockSpec((1, nb_t, 1, kv_w), lambda i, j: (i, j, 0, 0)),
        ]
        out_shape += [
            jax.ShapeDtypeStruct((n, kv_heads, nblk, MOBA_BLOCK, ATT_HEAD_DIM), BF16),
            jax.ShapeDtypeStruct((n, kv_heads, nblk, ATT_HEAD_DIM, MOBA_BLOCK), BF16),
            jax.ShapeDtypeStruct((n, nblk, 1, kv_w), F32),
        ]
    return pl.pallas_call(
        functools.partial(_inproj_kernel, offs=offs, heads_out=heads_out, kv_heads=kv_heads),
        grid=(n, t // tt),
        in_specs=[pl.BlockSpec((1, tt, d), lambda i, j: (i, j, 0)), mod_spec, mod_spec,
                  pl.BlockSpec((1, d), lambda i, j: (0, 0)),
                  pl.BlockSpec(w_pad.shape, lambda i, j: (0, 0))],
        out_specs=out_specs,
        out_shape=out_shape,
        compiler_params=_cparams(("parallel", "parallel")),
        name="in_proj_prompt" if heads_out else "in_proj_sample",
    )(x, sh, sc, norm_w.reshape(1, d), w_pad)


def _bias_tile_kernel(rb_ref, o_ref):
    kvh = pl.program_id(0)
    delta = pl.program_id(1)
    key = lax.broadcasted_iota(jnp.int32, (MOBA_BLOCK, MOBA_BLOCK), 0)
    qry = lax.broadcasted_iota(jnp.int32, (MOBA_BLOCK, MOBA_BLOCK), 1)
    bucket = _rel_bucket(delta * MOBA_BLOCK + qry - key)
    for g in range(ATT_GROUP):
        acc = jnp.zeros((MOBA_BLOCK, MOBA_BLOCK), F32)
        for t in range(REL_BUCKETS):
            acc = jnp.where(bucket == t, rb_ref[t, kvh * ATT_GROUP + g], acc)
        o_ref[0, 0, :, g * MOBA_BLOCK:(g + 1) * MOBA_BLOCK] = acc


def _bias_tiles(rel_bias, nblk):
    kv_heads = rel_bias.shape[1] // ATT_GROUP
    width = ATT_GROUP * MOBA_BLOCK
    return pl.pallas_call(
        _bias_tile_kernel,
        grid=(kv_heads, nblk),
        in_specs=[pl.BlockSpec(memory_space=pltpu.SMEM)],
        out_specs=pl.BlockSpec((1, 1, MOBA_BLOCK, width), lambda h, dlt: (h, dlt, 0, 0)),
        out_shape=jax.ShapeDtypeStruct((kv_heads, nblk, MOBA_BLOCK, width), F32),
        compiler_params=_cparams(("parallel", "parallel")),
        name="rel_bias_tiles",
    )(rel_bias)


def _bias_rows_kernel(tab_ref, o_ref, *, past, dec_seq):
    rows, width = o_ref.shape
    row = lax.broadcasted_iota(jnp.int32, (rows, width), 0)
    key = lax.broadcasted_iota(jnp.int32, (rows, width), 1)
    bucket = _rel_bucket(past + lax.rem(row, dec_seq) - key)
    acc = jnp.zeros((rows, width), F32)
    for t in range(REL_BUCKETS):
        acc = jnp.where(bucket == t, tab_ref[:, t:t + 1], acc)
    o_ref[...] = acc


def _bias_rows(rel_bias, past, dec_seq, width):
    heads = rel_bias.shape[1]
    rows = heads * dec_seq
    tab = jnp.repeat(rel_bias.T, dec_seq, axis=0)
    return pl.pallas_call(
        functools.partial(_bias_rows_kernel, past=past, dec_seq=dec_seq),
        out_shape=jax.ShapeDtypeStruct((rows, width), F32),
        name="rel_bias_rows",
    )(tab)


def _topk_rank_rows(gm, idx, n):
    rank = jnp.zeros(gm.shape, jnp.int32)
    for m in range(n):
        row = gm[m:m + 1, :]
        beats = (row > gm) | ((row == gm) & (m < idx))
        rank = rank + beats.astype(jnp.int32)
    return rank


def _attn_prompt_kernel(q_ref, kh_ref, vt_ref, km_ref, bias_ref, o_ref, sel_ref, *, nblk):
    qt = pl.program_id(2)
    q_t = q_ref[0].T
    tq = q_t.shape[1]
    width = ATT_GROUP * tq
    q_all = jnp.concatenate([q_t[g * ATT_HEAD_DIM:(g + 1) * ATT_HEAD_DIM, :] for g in range(ATT_GROUP)], axis=1)
    blk = lax.broadcasted_iota(jnp.int32, (nblk, width), 0)
    past = blk < qt
    gate = _dot(km_ref[0, 0], q_all, HIGHEST)
    rank = _topk_rank_rows(jnp.where(past, gate, NEG), blk, nblk)
    sel_ref[...] = (past & (rank < MOBA_TOPK)).astype(F32)
    qs = (q_all * (ATT_HEAD_DIM ** -0.5)).astype(BF16)
    key = lax.broadcasted_iota(jnp.int32, (MOBA_BLOCK, width), 0)
    qry = lax.rem(lax.broadcasted_iota(jnp.int32, (MOBA_BLOCK, width), 1), tq)

    def scores(kb):
        return _dot(kh_ref[0, 0, kb], qs)

    s = jnp.where(key <= qry, scores(qt) + bias_ref[0, 0], NEG)
    m0 = jnp.max(s, axis=0, keepdims=True)
    p = jnp.exp(s - m0)
    l0 = jnp.sum(p, axis=0, keepdims=True)
    acc0 = _dot(vt_ref[0, 0, qt], p.astype(BF16))

    def body(kb, carry):
        m, l, acc = carry
        s = scores(kb) + bias_ref[0, qt - kb]
        s = jnp.where(sel_ref[pl.ds(kb, 1), :] > 0.5, s, NEG)
        m_new = jnp.maximum(m, jnp.max(s, axis=0, keepdims=True))
        alpha = jnp.exp(m - m_new)
        p = jnp.exp(s - m_new)
        l = alpha * l + jnp.sum(p, axis=0, keepdims=True)
        acc = alpha * acc + _dot(vt_ref[0, 0, kb], p.astype(BF16))
        return m_new, l, acc

    _, l, acc = lax.fori_loop(0, qt, body, (m0, l0, acc0))
    out_t = acc / l
    o_ref[0] = jnp.concatenate([out_t[:, g * tq:(g + 1) * tq].T for g in range(ATT_GROUP)], axis=1)


def _attn_prompt(q, kh, vt, km, bias):
    n, t, q_w = q.shape
    kv_heads, nblk = kh.shape[1], kh.shape[2]
    gw = ATT_GROUP * ATT_HEAD_DIM
    return pl.pallas_call(
        functools.partial(_attn_prompt_kernel, nblk=nblk),
        grid=(n, kv_heads, nblk),
        in_specs=[
            pl.BlockSpec((1, MOBA_BLOCK, gw), lambda i, h, j: (i, j, h)),
            pl.BlockSpec((1, 1, nblk, MOBA_BLOCK, ATT_HEAD_DIM), lambda i, h, j: (i, h, 0, 0, 0)),
            pl.BlockSpec((1, 1, nblk, ATT_HEAD_DIM, MOBA_BLOCK), lambda i, h, j: (i, h, 0, 0, 0)),
            pl.BlockSpec((1, 1, nblk, ATT_HEAD_DIM), lambda i, h, j: (i, h, 0, 0)),
            pl.BlockSpec((1, nblk, MOBA_BLOCK, ATT_GROUP * MOBA_BLOCK), lambda i, h, j: (h, 0, 0, 0)),
        ],
        out_specs=pl.BlockSpec((1, MOBA_BLOCK, gw), lambda i, h, j: (i, j, h)),
        out_shape=jax.ShapeDtypeStruct((n, t, q_w), F32),
        scratch_shapes=[pltpu.VMEM((nblk, ATT_GROUP * MOBA_BLOCK), F32)],
        compiler_params=_cparams(("parallel", "parallel", "arbitrary")),
        name="moba_prompt",
    )(q, kh, vt, km, bias)


def _attn_sample_kernel(pt_ref, qc_ref, knew_ref, vnew_ref, bias_ref, ck_hbm, cv_hbm, o_ref,
                        kbuf, vbuf, sem, *, n_pages, dec_seq, kv_heads):
    s = pl.program_id(0)
    nseq = pl.num_programs(0)
    slot = lax.rem(s, 2)
    page = kbuf.shape[3]
    pages_per_blk = MOBA_BLOCK // page
    nblk = n_pages // pages_per_blk
    past_len = nblk * MOBA_BLOCK

    def start_fetch(seq, sl):
        def one(p, carry):
            pg = pt_ref[seq, p]
            pltpu.make_async_copy(ck_hbm.at[pg], kbuf.at[sl, p], sem.at[0, sl]).start()
            pltpu.make_async_copy(cv_hbm.at[pg], vbuf.at[sl, p], sem.at[1, sl]).start()
            return carry
        lax.fori_loop(0, n_pages, one, 0)

    @pl.when(s == 0)
    def _():
        start_fetch(0, 0)

    @pl.when(s + 1 < nseq)
    def _():
        start_fetch(s + 1, 1 - slot)

    pltpu.make_async_copy(ck_hbm.at[pl.ds(0, n_pages)], kbuf.at[slot], sem.at[0, slot]).wait()
    pltpu.make_async_copy(cv_hbm.at[pl.ds(0, n_pages)], vbuf.at[slot], sem.at[1, slot]).wait()

    qc = qc_ref[0]
    rows, kv_w = qc.shape
    scale = ATT_HEAD_DIM ** -0.5
    q_hi, q_lo = _split(qc)
    q_both = jnp.concatenate([q_hi, q_lo], axis=0)

    s_pages = []
    for p in range(n_pages):
        k_hi, k_lo = _split(kbuf[slot, p])
        t = _dot(q_both, k_hi)
        s_pages.append(t[:rows] + t[rows:] + _dot(q_hi, k_lo))

    lane = lax.broadcasted_iota(jnp.int32, (rows, LANES), 1)
    gate = jnp.zeros((rows, LANES), F32)
    for b in range(nblk):
        tot = s_pages[b * pages_per_blk]
        for p in range(b * pages_per_blk + 1, (b + 1) * pages_per_blk):
            tot = tot + s_pages[p]
        gate = jnp.where(lane == b, jnp.sum(tot, axis=1, keepdims=True) * (1.0 / MOBA_BLOCK), gate)
    rank = jnp.zeros((rows, LANES), jnp.int32)
    for m in range(nblk):
        col = gate[:, m:m + 1]
        rank = rank + ((col > gate) | ((col == gate) & (m < lane))).astype(jnp.int32)
    sel = rank < MOBA_TOPK

    masked = []
    for p in range(n_pages):
        b = p // pages_per_blk
        sp = s_pages[p] * scale + bias_ref[:, p * page:(p + 1) * page]
        masked.append(jnp.where(sel[:, b:b + 1], sp, NEG))
    n_new = knew_ref.shape[1]
    r_idx = lax.rem(lax.broadcasted_iota(jnp.int32, (rows, n_new), 0), dec_seq)
    j_idx = lax.broadcasted_iota(jnp.int32, (rows, n_new), 1)
    s_own = _dot_nt(q_hi, knew_ref[0].astype(BF16)) * scale + bias_ref[:, past_len:past_len + n_new]
    s_own = jnp.where(j_idx <= r_idx, s_own, NEG)
    m = jnp.max(s_own, axis=1, keepdims=True)
    for sp in masked:
        m = jnp.maximum(m, jnp.max(sp, axis=1, keepdims=True))
    p_own = jnp.exp(s_own - m)
    l = jnp.sum(p_own, axis=1, keepdims=True)
    acc = _dot(p_own.astype(BF16), vnew_ref[0].astype(BF16))
    for p, sp in enumerate(masked):
        prob = jnp.exp(sp - m)
        l = l + jnp.sum(prob, axis=1, keepdims=True)
        acc = acc + _dot_nt(prob.astype(BF16), vbuf[slot, p].astype(BF16))
    out = acc / l
    rpk = rows // kv_heads
    for h in range(kv_heads):
        o_ref[0, h * rpk:(h + 1) * rpk, :] = out[h * rpk:(h + 1) * rpk,
                                                 h * ATT_HEAD_DIM:(h + 1) * ATT_HEAD_DIM]


def _attn_sample(page_table, qc, k_new, v_new, bias, cache_k, cache_v, dec_seq, kv_heads):
    b, rows, kv_w = qc.shape
    n_pages = page_table.shape[1]
    page = cache_k.shape[2]
    n_new = k_new.shape[1]
    grid_spec = pltpu.PrefetchScalarGridSpec(
        num_scalar_prefetch=1,
        grid=(b,),
        in_specs=[
            pl.BlockSpec((1, rows, kv_w), lambda i, pt: (i, 0, 0)),
            pl.BlockSpec((1, n_new, kv_w), lambda i, pt: (i, 0, 0)),
            pl.BlockSpec((1, n_new, kv_w), lambda i, pt: (i, 0, 0)),
            pl.BlockSpec(bias.shape, lambda i, pt: (0, 0)),
            pl.BlockSpec(memory_space=pl.ANY),
            pl.BlockSpec(memory_space=pl.ANY),
        ],
        out_specs=pl.BlockSpec((1, rows, ATT_HEAD_DIM), lambda i, pt: (i, 0, 0)),
        scratch_shapes=[pltpu.VMEM((2, n_pages, kv_w, page), F32),
                        pltpu.VMEM((2, n_pages, kv_w, page), F32),
                        pltpu.SemaphoreType.DMA((2, 2))],
    )
    return pl.pallas_call(
        functools.partial(_attn_sample_kernel, n_pages=n_pages, dec_seq=dec_seq, kv_heads=kv_heads),
        grid_spec=grid_spec,
        out_shape=jax.ShapeDtypeStruct((b, rows, ATT_HEAD_DIM), F32),
        compiler_params=_cparams(("arbitrary",)),
        name="moba_sample",
    )(page_table, qc, k_new, v_new, bias, cache_k, cache_v)


def _gdn_gates(ba, alog, dt):
    return jax.nn.sigmoid(ba), -jnp.exp(alog) * _softplus(ba + dt)


def _gdn_prompt_kernel(x_ref, ba_ref, bat_ref, cw_ref, alog_ref, dt_ref, alogt_ref, dtt_ref,
                       o_ref, s_ref, ext_ref, *, heads):
    ct, cw = x_ref.shape[1], x_ref.shape[2]
    qk_w = heads * GDN_DK
    cs = GDN_CHUNK
    halo = SUBLANES

    @pl.when(pl.program_id(1) == 0)
    def _():
        ext_ref[0:halo, :] = jnp.zeros((halo, cw), F32)
        s_ref[...] = jnp.zeros(s_ref.shape, F32)

    x = x_ref[0]
    ext_ref[halo:halo + ct, :] = x
    w = cw_ref[...]
    conv = x * w[GDN_CONV - 1:GDN_CONV, :]
    for i in range(GDN_CONV - 1):
        conv = conv + ext_ref[pl.ds(halo - (GDN_CONV - 1) + i, ct), :] * w[i:i + 1, :]
    ext_ref[0:halo, :] = x[ct - halo:ct, :]
    act = _silu(conv)

    beta_f, g_f = _gdn_gates(ba_ref[0], alog_ref[...], dt_ref[...])
    _, g_t = _gdn_gates(bat_ref[0], alogt_ref[...], dtt_ref[...])

    r_i = lax.broadcasted_iota(jnp.int32, (cs, cs), 0)
    c_i = lax.broadcasted_iota(jnp.int32, (cs, cs), 1)
    tril = r_i >= c_i
    strict = r_i > c_i
    eye = (r_i == c_i).astype(F32)
    cum_l = tril.astype(F32)
    cum_u = (r_i <= c_i).astype(F32)

    qn, kn, vv = [], [], []
    for h in range(heads):
        qh = act[:, h * GDN_DK:(h + 1) * GDN_DK]
        kh = act[:, qk_w + h * GDN_DK:qk_w + (h + 1) * GDN_DK]
        qn.append(qh * lax.rsqrt(jnp.sum(qh * qh, axis=-1, keepdims=True) + EPS) * (GDN_DK ** -0.5))
        kn.append(kh * lax.rsqrt(jnp.sum(kh * kh, axis=-1, keepdims=True) + EPS))
        vv.append(act[:, 2 * qk_w + h * GDN_DK:2 * qk_w + (h + 1) * GDN_DK])

    n_chunks = ct // cs
    units = [(c, h) for c in range(n_chunks) for h in range(heads)]
    gc, gct = [], []
    for c in range(n_chunks):
        rows = slice(c * cs, (c + 1) * cs)
        gc.append(_dot(cum_l, g_f[rows, :], HIGHEST))
        gct.append(_dot(g_t[:, rows], cum_u, HIGHEST))
    q_u, k_u, kbeta_u, vbeta_u, gcc_u, egc_u, decay_u = {}, {}, {}, {}, {}, {}, {}
    for (c, h) in units:
        rows = slice(c * cs, (c + 1) * cs)
        beta = beta_f[rows, h:h + 1]
        gcc = gc[c][:, heads + h:heads + h + 1]
        gcr = gct[c][heads + h:heads + h + 1, :]
        q_u[c, h], k_u[c, h] = qn[h][rows], kn[h][rows]
        kbeta_u[c, h] = k_u[c, h] * beta
        vbeta_u[c, h] = vv[h][rows] * beta
        gcc_u[c, h], egc_u[c, h] = gcc, jnp.exp(gcc)
        decay_u[c, h] = jnp.where(tril, jnp.exp(jnp.where(tril, gcc - gcr, 0.0)), 0.0)
    a_u = {u: _mm3_nt(jnp.concatenate([kbeta_u[u], q_u[u]], axis=0), k_u[u]) for u in units}
    x_u = {u: -jnp.where(strict, a_u[u][:cs] * decay_u[u], 0.0) for u in units}
    intra_u = {u: jnp.where(tril, a_u[u][cs:] * decay_u[u], 0.0) for u in units}
    p_u = {u: eye + x_u[u] for u in units}
    xp_u = {u: _mm3(x_u[u], x_u[u]) for u in units}
    npow = 2
    while npow * 2 < cs:
        pr_u = {u: _mm3(jnp.concatenate([p_u[u], xp_u[u]], axis=0), xp_u[u]) for u in units}
        p_u = {u: p_u[u] + pr_u[u][:cs] for u in units}
        xp_u = {u: pr_u[u][cs:] for u in units}
        npow *= 2
    p_u = {u: p_u[u] + _mm3(p_u[u], xp_u[u]) for u in units}
    uw_u = {u: _mm3(p_u[u], jnp.concatenate([vbeta_u[u], kbeta_u[u] * egc_u[u]], axis=1)) for u in units}

    state = [s_ref[0, h] for h in range(heads)]
    for c in range(n_chunks):
        rows = slice(c * cs, (c + 1) * cs)
        ws = [_mm3(jnp.concatenate([uw_u[c, h][:, GDN_DK:], q_u[c, h] * egc_u[c, h]], axis=0), state[h])
              for h in range(heads)]
        v_new = [uw_u[c, h][:, :GDN_DK] - ws[h][:cs] for h in range(heads)]
        for h in range(heads):
            o_ref[0, rows, h * GDN_DK:(h + 1) * GDN_DK] = ws[h][cs:] + _mm3(intra_u[c, h], v_new[h])
        for h in range(heads):
            gcc = gcc_u[c, h]
            g_last = gcc[cs - 1:cs, :]
            state[h] = state[h] * jnp.exp(g_last) + _mm3_tn(k_u[c, h] * jnp.exp(g_last - gcc), v_new[h])
    for h in range(heads):
        s_ref[0, h] = state[h]


def _gdn_prompt(conv_pre, ba, conv_w, a_log, dt_bias, heads, ct):
    n, t, cw = conv_pre.shape
    bat = jnp.swapaxes(ba[:, :, :2 * heads], 1, 2)
    pad = jnp.zeros((heads,), F32)
    alog = jnp.concatenate([pad, a_log])
    dt = jnp.concatenate([pad, dt_bias])
    lane_pad = (0, LANES - 2 * heads)
    return pl.pallas_call(
        functools.partial(_gdn_prompt_kernel, heads=heads),
        grid=(n, t // ct),
        in_specs=[
            pl.BlockSpec((1, ct, cw), lambda i, j: (i, j, 0)),
            pl.BlockSpec((1, ct, LANES), lambda i, j: (i, j, 0)),
            pl.BlockSpec((1, 2 * heads, ct), lambda i, j: (i, 0, j)),
            pl.BlockSpec(conv_w.shape, lambda i, j: (0, 0)),
            pl.BlockSpec((1, LANES), lambda i, j: (0, 0)),
            pl.BlockSpec((1, LANES), lambda i, j: (0, 0)),
            pl.BlockSpec((2 * heads, 1), lambda i, j: (0, 0)),
            pl.BlockSpec((2 * heads, 1), lambda i, j: (0, 0)),
        ],
        out_specs=[pl.BlockSpec((1, ct, heads * GDN_DK), lambda i, j: (i, j, 0)),
                   pl.BlockSpec((1, heads, GDN_DK, GDN_DK), lambda i, j: (i, 0, 0, 0))],
        out_shape=[jax.ShapeDtypeStruct((n, t, heads * GDN_DK), F32),
                   jax.ShapeDtypeStruct((n, heads, GDN_DK, GDN_DK), F32)],
        scratch_shapes=[pltpu.VMEM((SUBLANES + ct, cw), F32)],
        compiler_params=_cparams(("parallel", "arbitrary")),
        name="gdn_prompt",
    )(conv_pre, ba, bat, conv_w, jnp.pad(alog, lane_pad).reshape(1, LANES),
      jnp.pad(dt, lane_pad).reshape(1, LANES), alog.reshape(2 * heads, 1), dt.reshape(2 * heads, 1))


def _gdn_sample_kernel(ext_ref, ba_ref, cw_ref, alog_ref, dt_ref, s_in_ref, o_ref, s_out_ref, *, heads):
    t_len = ba_ref.shape[0]
    gs = ext_ref.shape[1]
    qk_w = heads * GDN_DK
    w = cw_ref[...]
    q_t, k_t, v_r, beta_r, dec_r = [], [], [], [], []
    for t in range(t_len):
        conv = ext_ref[t] * w[0:1, :]
        for i in range(1, GDN_CONV):
            conv = conv + ext_ref[t + i] * w[i:i + 1, :]
        act = _silu(conv)
        beta_f, g_f = _gdn_gates(ba_ref[t], alog_ref[...], dt_ref[...])
        beta_r.append(beta_f)
        dec_r.append(jnp.exp(g_f))
        qs, ks, vs = [], [], []
        for h in range(heads):
            qh = act[:, h * GDN_DK:(h + 1) * GDN_DK]
            kh = act[:, qk_w + h * GDN_DK:qk_w + (h + 1) * GDN_DK]
            qn = qh * lax.rsqrt(jnp.sum(qh * qh, axis=-1, keepdims=True) + EPS) * (GDN_DK ** -0.5)
            kn = kh * lax.rsqrt(jnp.sum(kh * kh, axis=-1, keepdims=True) + EPS)
            qs.append(qn.T)
            ks.append(kn.T)
            vs.append(act[:, 2 * qk_w + h * GDN_DK:2 * qk_w + (h + 1) * GDN_DK])
        q_t.append(qs)
        k_t.append(ks)
        v_r.append(vs)
    for i in range(gs):
        for h in range(heads):
            st = s_in_ref[i, h]
            for t in range(t_len):
                st = st * dec_r[t][i:i + 1, heads + h:heads + h + 1]
                kcol = k_t[t][h][:, i:i + 1]
                ks_row = jnp.sum(st * kcol, axis=0, keepdims=True)
                delta = (v_r[t][h][i:i + 1, :] - ks_row) * beta_r[t][i:i + 1, h:h + 1]
                st = st + kcol * delta
                o_ref[t, i:i + 1, h * GDN_DK:(h + 1) * GDN_DK] = jnp.sum(
                    st * q_t[t][h][:, i:i + 1], axis=0, keepdims=True)
            s_out_ref[i, h] = st


def _gdn_sample(ext_tm, ba_tm, conv_w, a_log, dt_bias, state, heads, gs):
    t_len, b, _ = ba_tm.shape
    cw = ext_tm.shape[2]
    pad = jnp.zeros((heads,), F32)
    lane_pad = (0, LANES - 2 * heads)
    alog = jnp.pad(jnp.concatenate([pad, a_log]), lane_pad).reshape(1, LANES)
    dt = jnp.pad(jnp.concatenate([pad, dt_bias]), lane_pad).reshape(1, LANES)
    return pl.pallas_call(
        functools.partial(_gdn_sample_kernel, heads=heads),
        grid=(b // gs,),
        in_specs=[
            pl.BlockSpec((ext_tm.shape[0], gs, cw), lambda i: (0, i, 0)),
            pl.BlockSpec((t_len, gs, LANES), lambda i: (0, i, 0)),
            pl.BlockSpec(conv_w.shape, lambda i: (0, 0)),
            pl.BlockSpec((1, LANES), lambda i: (0, 0)),
            pl.BlockSpec((1, LANES), lambda i: (0, 0)),
            pl.BlockSpec((gs, heads, GDN_DK, GDN_DK), lambda i: (i, 0, 0, 0)),
        ],
        out_specs=[pl.BlockSpec((t_len, gs, heads * GDN_DK), lambda i: (0, i, 0)),
                   pl.BlockSpec((gs, heads, GDN_DK, GDN_DK), lambda i: (i, 0, 0, 0))],
        out_shape=[jax.ShapeDtypeStruct((t_len, b, heads * GDN_DK), F32),
                   jax.ShapeDtypeStruct(state.shape, F32)],
        compiler_params=_cparams(("parallel",)),
        name="gdn_sample",
    )(ext_tm, ba_tm, conv_w, alog, dt, state)


def _mid_kernel(x_ref, attn_ref, og_ref, z_ref, gta_ref, scf_ref, shf_ref, gw_ref, wout_ref, nf_ref,
                wr_ref, br_ref, x1_ref, h2_ref, gates_ref, topi_ref=None, *, heads, grouped):
    og = og_ref[0]
    z = z_ref[0]
    parts = [attn_ref[0]]
    for h in range(heads):
        o = og[:, h * GDN_DK:(h + 1) * GDN_DK]
        y = o * lax.rsqrt(jnp.mean(o * o, axis=-1, keepdims=True) + EPS) * gw_ref[...]
        parts.append(y * _silu(z[:, h * GDN_DK:(h + 1) * GDN_DK]))
    mix = jnp.concatenate(parts, axis=1).astype(BF16)
    x1 = x_ref[0] + gta_ref[0] * _dot(mix, wout_ref[...])
    x1_ref[0] = x1
    h2 = (x1 * lax.rsqrt(jnp.mean(x1 * x1, axis=-1, keepdims=True) + EPS) * nf_ref[...]
          * (1.0 + scf_ref[0]) + shf_ref[0])
    logits = _dot(h2, wr_ref[...], HIGHEST) + br_ref[...]
    lane = lax.broadcasted_iota(jnp.int32, logits.shape, 1)
    work = logits
    sel = jnp.zeros(logits.shape, jnp.bool_)
    vals, idxs = [], []
    for j in range(TOP_K):
        mx = jnp.max(work, axis=-1, keepdims=True)
        idx = jnp.min(jnp.where(work == mx, lane, LANES), axis=-1, keepdims=True)
        pick = lane == idx
        vals.append(mx)
        idxs.append(idx)
        sel = sel | pick
        work = jnp.where(pick, -jnp.inf, work)
    if grouped:
        tt = h2.shape[0]
        for s in range(h2.shape[1] // LANES):
            h2_ref[0, pl.ds(s, tt, stride=h2.shape[1] // LANES), :] = h2[:, s * LANES:(s + 1) * LANES]
        ex = [jnp.exp(v - vals[0]) for v in vals]
        denom = ex[0]
        for v in ex[1:]:
            denom = denom + v
        topi = jnp.zeros(logits.shape, jnp.int32)
        topw = jnp.zeros(logits.shape, F32)
        for j in range(TOP_K):
            topi = jnp.where(lane == j, idxs[j], topi)
            topw = jnp.where(lane == j, ex[j] / denom, topw)
        gates_ref[0] = topw
        topi_ref[0] = topi
    else:
        h2_ref[0] = h2.astype(BF16)
        e = jnp.where(sel, jnp.exp(logits - vals[0]), 0.0)
        gates_ref[0] = e / jnp.sum(e, axis=-1, keepdims=True)


def _mid(x, attn, og, z, gta, scf, shf, gdn_norm_w, w_out_bf, norm_ffn_w, wr_pad, br_pad, tt, heads, grouped):
    n, t, d = x.shape
    slab = d // LANES
    per_token = gta.shape[1] != 1
    mod_spec = (pl.BlockSpec((1, tt, d), lambda i, j: (i, j, 0)) if per_token
                else pl.BlockSpec((1, 1, d), lambda i, j: (i, 0, 0)))

    def tok(width):
        return pl.BlockSpec((1, tt, width), lambda i, j: (i, j, 0))

    def full(a):
        return pl.BlockSpec(a.shape, lambda i, j: (0,) * a.ndim)

    gw = gdn_norm_w.reshape(1, -1)
    nf = norm_ffn_w.reshape(1, d)
    if grouped:
        out_specs = [tok(d), pl.BlockSpec((1, tt * slab, LANES), lambda i, j: (i, j, 0)), tok(LANES), tok(LANES)]
        out_shape = [jax.ShapeDtypeStruct((n, t, d), F32), jax.ShapeDtypeStruct((n, t * slab, LANES), F32),
                     jax.ShapeDtypeStruct((n, t, LANES), F32), jax.ShapeDtypeStruct((n, t, LANES), jnp.int32)]
    else:
        out_specs = [tok(d), tok(d), tok(LANES)]
        out_shape = [jax.ShapeDtypeStruct((n, t, d), F32), jax.ShapeDtypeStruct((n, t, d), BF16),
                     jax.ShapeDtypeStruct((n, t, LANES), F32)]
    return pl.pallas_call(
        functools.partial(_mid_kernel, heads=heads, grouped=grouped),
        grid=(n, t // tt),
        in_specs=[tok(d), tok(attn.shape[2]), tok(og.shape[2]), tok(z.shape[2]), mod_spec, mod_spec, mod_spec,
                  full(gw), full(w_out_bf), full(nf), full(wr_pad), full(br_pad)],
        out_specs=out_specs,
        out_shape=out_shape,
        compiler_params=_cparams(("parallel", "parallel")),
        name="out_proj_router_grouped" if grouped else "out_proj_router",
    )(x, attn, og, z, gta, scf, shf, gw, w_out_bf, nf, wr_pad, br_pad)


def _expert_ffn(h, wup_ref, bup_ref, wdn_ref, bdn_ref, ff_chunk):
    d_ff = wdn_ref.shape[1]
    y = jnp.zeros((h.shape[0], wdn_ref.shape[2]), F32)
    for c in range(d_ff // ff_chunk):
        lo, hi = c * ff_chunk, (c + 1) * ff_chunk
        x_glu = _dot(h, wup_ref[0, :, lo:hi]) + bup_ref[0, :, lo:hi]
        x_lin = _dot(h, wup_ref[0, :, d_ff + lo:d_ff + hi]) + bup_ref[0, :, d_ff + lo:d_ff + hi]
        x_glu = jnp.minimum(x_glu, SWIGLU_LIMIT)
        x_lin = jnp.clip(x_lin, -SWIGLU_LIMIT, SWIGLU_LIMIT)
        act = x_glu * jax.nn.sigmoid(SWIGLU_ALPHA * x_glu) * (x_lin + 1.0)
        y = y + _dot(act.astype(BF16), wdn_ref[0, lo:hi, :])
    return y + bdn_ref[0]


def _moe_kernel(h2_ref, gates_ref, x1_ref, gtf_ref, wup_ref, bup_ref, wdn_ref, bdn_ref, nw_ref,
                y_ref, acc_ref, *, ff_chunk):
    e = pl.program_id(2)
    n_exp = pl.num_programs(2)

    @pl.when(e == 0)
    def _():
        acc_ref[...] = jnp.zeros(acc_ref.shape, F32)

    gates = gates_ref[0]
    lane = lax.broadcasted_iota(jnp.int32, gates.shape, 1)
    gcol = jnp.sum(jnp.where(lane == e, gates, 0.0), axis=-1, keepdims=True)
    acc_ref[...] += gcol * _expert_ffn(h2_ref[0], wup_ref, bup_ref, wdn_ref, bdn_ref, ff_chunk)

    @pl.when(e == n_exp - 1)
    def _():
        x = x1_ref[0] + gtf_ref[0] * acc_ref[...]
        y_ref[0] = x * lax.rsqrt(jnp.mean(x * x, axis=-1, keepdims=True) + EPS) * nw_ref[...]


def _moe(h2, gates, x1, gtf, w_up_bf, b_up, w_dn_bf, b_dn, norm_final_w, tm):
    n, t, d = x1.shape
    n_exp, _, ff2 = w_up_bf.shape
    d_ff = ff2 // 2
    per_token = gtf.shape[1] != 1
    mod_spec = (pl.BlockSpec((1, tm, d), lambda i, j, e: (i, j, 0)) if per_token
                else pl.BlockSpec((1, 1, d), lambda i, j, e: (i, 0, 0)))

    def tok(width):
        return pl.BlockSpec((1, tm, width), lambda i, j, e: (i, j, 0))

    return pl.pallas_call(
        functools.partial(_moe_kernel, ff_chunk=min(512, d_ff)),
        grid=(n, t // tm, n_exp),
        in_specs=[tok(d), tok(LANES), tok(d), mod_spec,
                  pl.BlockSpec((1, d, ff2), lambda i, j, e: (e, 0, 0)),
                  pl.BlockSpec((1, 1, ff2), lambda i, j, e: (e, 0, 0)),
                  pl.BlockSpec((1, d_ff, d), lambda i, j, e: (e, 0, 0)),
                  pl.BlockSpec((1, 1, d), lambda i, j, e: (e, 0, 0)),
                  pl.BlockSpec((1, d), lambda i, j, e: (0, 0))],
        out_specs=tok(d),
        out_shape=jax.ShapeDtypeStruct((n, t, d), F32),
        scratch_shapes=[pltpu.VMEM((tm, d), F32)],
        compiler_params=_cparams(("parallel", "parallel", "arbitrary")),
        name="moe_ffn",
    )(h2, gates, x1, gtf, w_up_bf, b_up.reshape(n_exp, 1, ff2), w_dn_bf, b_dn.reshape(n_exp, 1, d),
      norm_final_w.reshape(1, d))


MOE_GROUP = 2048
MOE_CHUNK = 288
MOE_SCATTER_BATCH = 8


def _moe_grouped_kernel(cnt_ref, off_ref, tok_ref, wgt_ref, xg_ref, wup_ref, bup_ref, wdn_ref, bdn_ref,
                        y_ref, xbuf, obuf, *, group, chunk, ff_chunk):
    g = pl.program_id(0)
    e = pl.program_id(1)
    slab = xbuf.shape[0] // chunk

    @pl.when(e == 0)
    def _():
        y_ref[...] = jnp.zeros(y_ref.shape, F32)

    seg = g * pl.num_programs(1) + e
    off = off_ref[seg]
    end = off + cnt_ref[seg]

    def one_chunk(j, carry):
        r0 = off + j * chunk
        for i in range(chunk):
            tok = tok_ref[0, 0, r0 + i]
            xbuf[pl.ds(i * slab, slab), :] = xg_ref[0, pl.ds(pl.multiple_of(tok * slab, slab), slab), :]
        x = jnp.concatenate([xbuf[pl.ds(s, chunk, stride=slab), :] for s in range(slab)], axis=1)
        y = _expert_ffn(x.astype(BF16), wup_ref, bup_ref, wdn_ref, bdn_ref, ff_chunk)
        for s in range(slab):
            obuf[pl.ds(s, chunk, stride=slab), :] = y[:, s * LANES:(s + 1) * LANES]
        for b0 in range(0, chunk, MOE_SCATTER_BATCH):
            rows, wgts = [], []
            for i in range(b0, b0 + MOE_SCATTER_BATCH):
                valid = r0 + i < end
                tok = jnp.where(valid, tok_ref[0, 0, r0 + i], group)
                rows.append(pl.multiple_of(tok * slab, slab))
                wgts.append(jnp.where(valid, wgt_ref[0, 0, r0 + i], 0.0))
            cur = [y_ref[0, pl.ds(r, slab), :] for r in rows]
            new = [cur[k] + wgts[k] * obuf[pl.ds((b0 + k) * slab, slab), :] for k in range(MOE_SCATTER_BATCH)]
            for k in range(MOE_SCATTER_BATCH):
                y_ref[0, pl.ds(rows[k], slab), :] = new[k]
        return carry

    lax.fori_loop(0, (end - off + chunk - 1) // chunk, one_chunk, 0)


def _moe_grouped(h2_slabs, topw, topi, w_up_bf, b_up, w_dn_bf, b_dn, group):
    m = topw.shape[0]
    slab = h2_slabs.shape[0] // m
    n_exp, d, ff2 = w_up_bf.shape
    d_ff = ff2 // 2
    ngroups = m // group
    rows = group * TOP_K
    eid = topi[:, :TOP_K].reshape(ngroups, rows)
    order = jnp.argsort(eid, axis=1, stable=True)
    tok_sorted = (order // TOP_K).astype(jnp.int32)
    wgt_sorted = jnp.take_along_axis(topw[:, :TOP_K].reshape(ngroups, rows), order, axis=1)
    counts = jnp.sum((eid[:, :, None] == jnp.arange(n_exp, dtype=jnp.int32)).astype(jnp.int32), axis=1)
    offs = jnp.cumsum(counts, axis=1) - counts
    tail = ((0, 0), (0, MOE_CHUNK))
    tok_sorted = jnp.pad(tok_sorted, tail).reshape(ngroups, 1, rows + MOE_CHUNK)
    wgt_sorted = jnp.pad(wgt_sorted, tail).reshape(ngroups, 1, rows + MOE_CHUNK)

    grid_spec = pltpu.PrefetchScalarGridSpec(
        num_scalar_prefetch=2,
        grid=(ngroups, n_exp),
        in_specs=[
            pl.BlockSpec((1, 1, rows + MOE_CHUNK), lambda g, e, c, o: (g, 0, 0), memory_space=pltpu.SMEM),
            pl.BlockSpec((1, 1, rows + MOE_CHUNK), lambda g, e, c, o: (g, 0, 0), memory_space=pltpu.SMEM),
            pl.BlockSpec((1, group * slab, LANES), lambda g, e, c, o: (g, 0, 0), pipeline_mode=pl.Buffered(1)),
            pl.BlockSpec((1, d, ff2), lambda g, e, c, o: (e, 0, 0)),
            pl.BlockSpec((1, 1, ff2), lambda g, e, c, o: (e, 0, 0)),
            pl.BlockSpec((1, d_ff, d), lambda g, e, c, o: (e, 0, 0)),
            pl.BlockSpec((1, 1, d), lambda g, e, c, o: (e, 0, 0)),
        ],
        out_specs=pl.BlockSpec((1, (group + 1) * slab, LANES), lambda g, e, c, o: (g, 0, 0)),
        scratch_shapes=[pltpu.VMEM((MOE_CHUNK * slab, LANES), F32), pltpu.VMEM((MOE_CHUNK * slab, LANES), F32)],
    )
    return pl.pallas_call(
        functools.partial(_moe_grouped_kernel, group=group, chunk=MOE_CHUNK, ff_chunk=min(512, d_ff)),
        grid_spec=grid_spec,
        out_shape=jax.ShapeDtypeStruct((ngroups, (group + 1) * slab, LANES), F32),
        compiler_params=pltpu.CompilerParams(dimension_semantics=("parallel", "arbitrary"),
                                             vmem_limit_bytes=MOE_VMEM_LIMIT),
        name="moe_grouped",
    )(counts.reshape(-1), offs.reshape(-1), tok_sorted, wgt_sorted,
      h2_slabs.reshape(ngroups, group * slab, LANES), w_up_bf, b_up.reshape(n_exp, 1, ff2), w_dn_bf,
      b_dn.reshape(n_exp, 1, d))


def _moe_finish_kernel(moe_ref, x1_ref, gtf_ref, nw_ref, y_ref):
    tt, d = x1_ref.shape[1], x1_ref.shape[2]
    slab = d // LANES
    moe = jnp.concatenate([moe_ref[0, pl.ds(s, tt, stride=slab), :] for s in range(slab)], axis=1)
    x = x1_ref[0] + gtf_ref[0] * moe
    y_ref[0] = x * lax.rsqrt(jnp.mean(x * x, axis=-1, keepdims=True) + EPS) * nw_ref[...]


def _moe_finish(moe_slabs, x1, gtf, norm_final_w, group, tt):
    n, t, d = x1.shape
    slab = d // LANES
    per_group = group // tt
    steps = t // tt
    return pl.pallas_call(
        _moe_finish_kernel,
        grid=(n, steps),
        in_specs=[
            pl.BlockSpec((1, tt * slab, LANES),
                         lambda i, j: ((i * steps + j) // per_group, (i * steps + j) % per_group, 0)),
            pl.BlockSpec((1, tt, d), lambda i, j: (i, j, 0)),
            pl.BlockSpec((1, 1, d), lambda i, j: (i, 0, 0)),
            pl.BlockSpec((1, d), lambda i, j: (0, 0)),
        ],
        out_specs=pl.BlockSpec((1, tt, d), lambda i, j: (i, j, 0)),
        out_shape=jax.ShapeDtypeStruct((n, t, d), F32),
        compiler_params=_cparams(("parallel", "parallel")),
        name="moe_finish",
    )(moe_slabs, x1, gtf, norm_final_w.reshape(1, d))


def _pick_tile(t, pref):
    tile = min(t, pref)
    assert t % tile == 0, (t, tile)
    return tile


def kernel(x_prompt, x_sample, cache_k, cache_v, state_conv, state_ssm, page_table, c_prompt, c_sample, w_ada, b_ada, norm_attn_w, norm_ffn_w, norm_final_w, w_in, rel_bias, conv_w, a_log, dt_bias, gdn_norm_w, w_out, w_router, b_router, w_up, b_up, w_down, b_down):
    depth = w_in.shape[0]
    assert depth == 1, "single-layer step"
    nb, seq, d = x_prompt.shape
    db, dec_seq, _ = x_sample.shape
    n_pool, page, kv_heads, dh = cache_k.shape[1:]
    assert dh == ATT_HEAD_DIM
    att_heads = rel_bias.shape[1]
    assert att_heads == kv_heads * ATT_GROUP
    gdn_heads = a_log.shape[1]
    n_pages = page_table.shape[1]
    past = n_pages * page
    q_w = att_heads * dh
    kv_w = kv_heads * dh
    gdn_cw = conv_w.shape[2]
    z_w = gdn_heads * GDN_DK
    assert gdn_cw == 3 * z_w
    in_w = q_w + 2 * kv_w + gdn_cw + z_w + 2 * gdn_heads
    assert w_in.shape[2] == in_w and in_w - 2 * gdn_heads == (in_w // LANES) * LANES
    assert seq % MOBA_BLOCK == 0 and past % MOBA_BLOCK == 0 and MOBA_BLOCK % page == 0
    assert dec_seq <= MOBA_BLOCK and dec_seq >= GDN_CONV - 1 and dec_seq <= SUBLANES
    n_exp = w_router.shape[2]
    nblk = seq // MOBA_BLOCK
    widths = (q_w, kv_w, gdn_cw, z_w)

    n_mod = nb + db
    n_mod_pad = -(-n_mod // SUBLANES) * SUBLANES
    c_all = jnp.pad(jnp.concatenate([c_prompt, c_sample], axis=0), ((0, n_mod_pad - n_mod), (0, 0)))
    mod = _ada(c_all, w_ada[0], b_ada[0])
    mod_p = [mod[:nb, i * d:(i + 1) * d].reshape(nb, 1, d) for i in range(6)]
    m_s = db * dec_seq
    mod_s = [jnp.broadcast_to(mod[nb:n_mod, None, i * d:(i + 1) * d], (db, dec_seq, d)).reshape(1, m_s, d)
             for i in range(6)]

    w_in_pad = jnp.pad(w_in[0], ((0, 0), (0, LANES - 2 * gdn_heads))).astype(BF16)
    w_out_bf = w_out[0].astype(BF16)
    wr_pad = jnp.pad(w_router[0], ((0, 0), (0, LANES - n_exp)))
    br_pad = jnp.pad(b_router[0], (0, LANES - n_exp), constant_values=NEG).reshape(1, LANES)
    w_up_bf = w_up[0].astype(BF16)
    w_dn_bf = w_down[0].astype(BF16)

    tt_p = _pick_tile(seq, 512)
    q_p, k_p, v_p, conv_p, z_p, ba_p, kh_p, vt_p, km_p = _inproj(
        x_prompt, mod_p[0], mod_p[1], norm_attn_w[0], w_in_pad, widths, tt_p, True)
    km_h = km_p.reshape(nb, nblk, kv_heads, dh).transpose(0, 2, 1, 3)
    bias_t = _bias_tiles(rel_bias, nblk)
    attn_p = _attn_prompt(q_p, kh_p, vt_p, km_h, bias_t)
    og_p, ssm_p = _gdn_prompt(conv_p, ba_p, conv_w[0], a_log[0], dt_bias[0], gdn_heads, _pick_tile(seq, 256))
    x1_p, h2_p, topw_p, topi_p = _mid(x_prompt, attn_p, og_p, z_p, mod_p[2], mod_p[4], mod_p[3], gdn_norm_w[0],
                                      w_out_bf, norm_ffn_w[0], wr_pad, br_pad, tt_p, gdn_heads, True)
    m_p = nb * seq
    group = _pick_tile(m_p, MOE_GROUP)
    assert group % tt_p == 0 and seq % tt_p == 0
    moe_p = _moe_grouped(h2_p.reshape(m_p * (d // LANES), LANES), topw_p.reshape(m_p, LANES),
                         topi_p.reshape(m_p, LANES), w_up_bf, b_up[0], w_dn_bf, b_down[0], group)
    y_p = _moe_finish(moe_p, x1_p, mod_p[5], norm_final_w, group, tt_p)

    tt_s = _pick_tile(m_s, 512)
    q_s, k_s, v_s, conv_s, z_s, ba_s = _inproj(
        x_sample.reshape(1, m_s, d), mod_s[0], mod_s[1], norm_attn_w[0], w_in_pad, widths, tt_s, False)
    q4 = q_s.reshape(db, dec_seq, kv_heads, ATT_GROUP, dh).transpose(0, 2, 3, 1, 4)
    q4 = q4.reshape(db, kv_heads, ATT_GROUP * dec_seq, 1, dh)
    eye = jnp.eye(kv_heads, dtype=F32).reshape(1, kv_heads, 1, kv_heads, 1)
    qc = (q4 * eye).reshape(db, att_heads * dec_seq, kv_w)
    new_pad = ((0, 0), (0, SUBLANES - dec_seq), (0, 0))
    k_new = jnp.pad(k_s.reshape(db, dec_seq, kv_w), new_pad)
    v_new = jnp.pad(v_s.reshape(db, dec_seq, kv_w), new_pad)
    bias_r = _bias_rows(rel_bias, past, dec_seq, past + MOBA_BLOCK)
    ck_t = cache_k[0].transpose(0, 2, 3, 1).reshape(n_pool, kv_w, page)
    cv_t = cache_v[0].transpose(0, 2, 3, 1).reshape(n_pool, kv_w, page)
    attn_rows = _attn_sample(page_table, qc, k_new, v_new, bias_r, ck_t, cv_t, dec_seq, kv_heads)
    attn_s = attn_rows.reshape(db, att_heads, dec_seq, dh).transpose(0, 2, 1, 3).reshape(1, m_s, q_w)

    conv_s3 = conv_s.reshape(db, dec_seq, gdn_cw)
    ext = jnp.concatenate([state_conv[0], conv_s3], axis=1)
    gs = _pick_tile(db, SUBLANES)
    og_tm, ssm_s = _gdn_sample(jnp.swapaxes(ext, 0, 1), jnp.swapaxes(ba_s.reshape(db, dec_seq, LANES), 0, 1),
                               conv_w[0], a_log[0], dt_bias[0], state_ssm[0], gdn_heads, gs)
    og_s = jnp.swapaxes(og_tm, 0, 1).reshape(1, m_s, z_w)
    x1_s, h2_s, gates_s = _mid(x_sample.reshape(1, m_s, d), attn_s, og_s, z_s, mod_s[2], mod_s[4], mod_s[3],
                               gdn_norm_w[0], w_out_bf, norm_ffn_w[0], wr_pad, br_pad, tt_s, gdn_heads, False)
    y_s = _moe(h2_s, gates_s, x1_s, mod_s[5], w_up_bf, b_up[0], w_dn_bf, b_down[0], norm_final_w, tt_s)

    return (
        y_p,
        y_s.reshape(db, dec_seq, d),
        k_p.reshape(1, nb, seq, kv_heads, dh),
        v_p.reshape(1, nb, seq, kv_heads, dh),
        conv_p[:, seq - (GDN_CONV - 1):, :][None],
        ssm_p[None],
        k_s.reshape(1, db, dec_seq, kv_heads, dh),
        v_s.reshape(1, db, dec_seq, kv_heads, dh),
        ext[:, dec_seq:, :][None],
        ssm_s[None],
    )
```

```python
import functools
import math

import jax
import jax.numpy as jnp
from jax import lax
from jax.experimental import pallas as pl
from jax.experimental.pallas import tpu as pltpu

F32 = jnp.float32
BF16 = jnp.bfloat16
HIGHEST = lax.Precision.HIGHEST

ATT_HEAD_DIM = 64
ATT_GROUP = 2
MOBA_BLOCK = 256
MOBA_TOPK = 3
REL_BUCKETS = 32
REL_MAX_DIST = 4096
GDN_DK = 128
GDN_CONV = 4
GDN_CHUNK = 64
TOP_K = 4
SWIGLU_LIMIT = 7.0
SWIGLU_ALPHA = 1.702
EPS = 1e-6
NEG = -1e30

LANES = 128
SUBLANES = 8
VT_ROWS = ATT_HEAD_DIM + SUBLANES
LOG2E = math.log2(math.e)
VMEM_LIMIT = 48 * 1024 * 1024
MOE_VMEM_LIMIT = 56 * 1024 * 1024


def _cparams(sem):
    return pltpu.CompilerParams(dimension_semantics=sem, vmem_limit_bytes=VMEM_LIMIT)


def _dot(a, b, precision=None):
    return jnp.dot(a, b, precision=precision, preferred_element_type=F32)


def _dot_nt(a, b, precision=None):
    return lax.dot_general(a, b, (((1,), (1,)), ((), ())), precision=precision,
                           preferred_element_type=F32)


def _dot_tn(a, b, precision=None):
    return lax.dot_general(a, b, (((0,), (0,)), ((), ())), precision=precision,
                           preferred_element_type=F32)


def _split(x):
    hi = x.astype(BF16)
    return hi, (x - hi.astype(F32)).astype(BF16)


def _mm3(a, b):
    m = a.shape[0]
    ah, al = _split(a)
    bh, bl = _split(b)
    t = _dot(jnp.concatenate([ah, al], axis=0), bh)
    return t[:m] + t[m:] + _dot(ah, bl)


def _mm3_nt(a, b):
    m = a.shape[0]
    ah, al = _split(a)
    bh, bl = _split(b)
    t = _dot_nt(jnp.concatenate([ah, al], axis=0), bh)
    return t[:m] + t[m:] + _dot_nt(ah, bl)


def _silu(x):
    return x * jax.nn.sigmoid(x)


def _softplus(x):
    return jnp.maximum(x, 0.0) + jnp.log1p(jnp.exp(-jnp.abs(x)))


def _rel_bucket(dist):
    n = jnp.maximum(dist, 0)
    max_exact = REL_BUCKETS // 2
    nf = jnp.maximum(n, max_exact).astype(F32)
    large = max_exact + (jnp.log(nf / max_exact) / math.log(REL_MAX_DIST / max_exact)
                         * (REL_BUCKETS - max_exact)).astype(jnp.int32)
    return jnp.where(n < max_exact, n, jnp.minimum(large, REL_BUCKETS - 1))


def _ada_kernel(c_ref, w_ref, b_ref, o_ref):
    o_ref[...] = _dot(_silu(c_ref[...]), w_ref[...], HIGHEST) + b_ref[...]


def _ada(c, w_ada, b_ada):
    m, d = c.shape
    n = w_ada.shape[1]
    tn = d
    return pl.pallas_call(
        _ada_kernel,
        grid=(n // tn,),
        in_specs=[pl.BlockSpec((m, d), lambda j: (0, 0)),
                  pl.BlockSpec((d, tn), lambda j: (0, j)),
                  pl.BlockSpec((1, tn), lambda j: (0, j))],
        out_specs=pl.BlockSpec((m, tn), lambda j: (0, j)),
        out_shape=jax.ShapeDtypeStruct((m, n), F32),
        compiler_params=_cparams(("parallel",)),
        name="ada_modulation",
    )(c, w_ada, b_ada.reshape(1, n))


def _inproj_kernel(x_ref, sh_ref, sc_ref, nw_ref, w_ref, *out_refs, offs, heads_out, kv_heads):
    x = x_ref[0]
    xn = x * lax.rsqrt(jnp.mean(x * x, axis=-1, keepdims=True) + EPS) * nw_ref[...]
    h = (xn * (1.0 + sc_ref[0]) + sh_ref[0]).astype(BF16)
    o0, o1, o2, o3, o4, o5 = offs

    def proj(lo, hi):
        return _dot(h, w_ref[:, lo:hi])

    q_ref, k_ref, v_ref, conv_ref, z_ref, ba_ref = out_refs[:6]
    q_ref[0] = proj(o0, o1)
    k = proj(o1, o2)
    v = proj(o2, o3)
    k_ref[0] = k
    v_ref[0] = v
    conv_ref[0] = proj(o3, o4)
    z_ref[0] = proj(o4, o5)
    ba_ref[0] = proj(o5, o5 + LANES)
    if heads_out:
        kh_ref, vt_ref, km_ref = out_refs[6:]
        vt = v.T
        for b in range(x.shape[0] // MOBA_BLOCK):
            rows = slice(b * MOBA_BLOCK, (b + 1) * MOBA_BLOCK)
            kb = k[rows]
            km_ref[0, b] = jnp.mean(kb, axis=0, keepdims=True)
            for hh in range(kv_heads):
                cols = slice(hh * ATT_HEAD_DIM, (hh + 1) * ATT_HEAD_DIM)
                kh_ref[0, hh, b] = kb[:, cols].astype(BF16)
                vt_ref[0, hh, b, 0:ATT_HEAD_DIM, :] = vt[cols, rows].astype(BF16)
                vt_ref[0, hh, b, ATT_HEAD_DIM:VT_ROWS, :] = jnp.ones((VT_ROWS - ATT_HEAD_DIM, MOBA_BLOCK), BF16)


def _inproj(x, sh, sc, norm_w, w_pad, widths, tt, heads_out):
    n, t, d = x.shape
    per_token = sh.shape[1] != 1
    q_w, kv_w, conv_w_, z_w = widths
    offs = (0, q_w, q_w + kv_w, q_w + 2 * kv_w, q_w + 2 * kv_w + conv_w_, q_w + 2 * kv_w + conv_w_ + z_w)
    kv_heads = kv_w // ATT_HEAD_DIM
    nb_t = tt // MOBA_BLOCK
    mod_spec = (pl.BlockSpec((1, tt, d), lambda i, j: (i, j, 0)) if per_token
                else pl.BlockSpec((1, 1, d), lambda i, j: (i, 0, 0)))

    def tok(width):
        return pl.BlockSpec((1, tt, width), lambda i, j: (i, j, 0))

    out_specs = [tok(q_w), tok(kv_w), tok(kv_w), tok(conv_w_), tok(z_w), tok(LANES)]
    out_shape = [jax.ShapeDtypeStruct((n, t, w), F32) for w in (q_w, kv_w, kv_w, conv_w_, z_w, LANES)]
    if heads_out:
        nblk = t // MOBA_BLOCK
        out_specs += [
            pl.BlockSpec((1, kv_heads, nb_t, MOBA_BLOCK, ATT_HEAD_DIM), lambda i, j: (i, 0, j, 0, 0)),
            pl.BlockSpec((1, kv_heads, nb_t, VT_ROWS, MOBA_BLOCK), lambda i, j: (i, 0, j, 0, 0)),
            pl.BlockSpec((1, nb_t, 1, kv_w), lambda i, j: (i, j, 0, 0)),
        ]
        out_shape += [
            jax.ShapeDtypeStruct((n, kv_heads, nblk, MOBA_BLOCK, ATT_HEAD_DIM), BF16),
            jax.ShapeDtypeStruct((n, kv_heads, nblk, VT_ROWS, MOBA_BLOCK), BF16),
            jax.ShapeDtypeStruct((n, nblk, 1, kv_w), F32),
        ]
    return pl.pallas_call(
        functools.partial(_inproj_kernel, offs=offs, heads_out=heads_out, kv_heads=kv_heads),
        grid=(n, t // tt),
        in_specs=[pl.BlockSpec((1, tt, d), lambda i, j: (i, j, 0)), mod_spec, mod_spec,
                  pl.BlockSpec((1, d), lambda i, j: (0, 0)),
                  pl.BlockSpec(w_pad.shape, lambda i, j: (0, 0))],
        out_specs=out_specs,
        out_shape=out_shape,
        compiler_params=_cparams(("parallel", "parallel")),
        name="in_proj_prompt" if heads_out else "in_proj_sample",
    )(x, sh, sc, norm_w.reshape(1, d), w_pad)


def _bias_tile_kernel(rb_ref, o_ref):
    kvh = pl.program_id(0)
    delta = pl.num_programs(1) - 1 - pl.program_id(1)
    key = lax.broadcasted_iota(jnp.int32, (MOBA_BLOCK, MOBA_BLOCK), 0)
    qry = lax.broadcasted_iota(jnp.int32, (MOBA_BLOCK, MOBA_BLOCK), 1)
    bucket = _rel_bucket(delta * MOBA_BLOCK + qry - key)
    for g in range(ATT_GROUP):
        acc = jnp.zeros((MOBA_BLOCK, MOBA_BLOCK), F32)
        for t in range(REL_BUCKETS):
            acc = jnp.where(bucket == t, rb_ref[t, kvh * ATT_GROUP + g], acc)
        o_ref[0, 0, :, g * MOBA_BLOCK:(g + 1) * MOBA_BLOCK] = acc * LOG2E


def _bias_tiles(rel_bias, nblk):
    kv_heads = rel_bias.shape[1] // ATT_GROUP
    width = ATT_GROUP * MOBA_BLOCK
    return pl.pallas_call(
        _bias_tile_kernel,
        grid=(kv_heads, nblk),
        in_specs=[pl.BlockSpec(memory_space=pltpu.SMEM)],
        out_specs=pl.BlockSpec((1, 1, MOBA_BLOCK, width), lambda h, dlt: (h, dlt, 0, 0)),
        out_shape=jax.ShapeDtypeStruct((kv_heads, nblk, MOBA_BLOCK, width), F32),
        compiler_params=_cparams(("parallel", "parallel")),
        name="rel_bias_tiles",
    )(rel_bias)


def _bias_rows_kernel(tab_ref, o_ref, *, past, dec_seq):
    rows, width = o_ref.shape
    row = lax.broadcasted_iota(jnp.int32, (rows, width), 0)
    key = lax.broadcasted_iota(jnp.int32, (rows, width), 1)
    bucket = _rel_bucket(past + lax.rem(row, dec_seq) - key)
    acc = jnp.zeros((rows, width), F32)
    for t in range(REL_BUCKETS):
        acc = jnp.where(bucket == t, tab_ref[:, t:t + 1], acc)
    o_ref[...] = acc


def _bias_rows(rel_bias, past, dec_seq, width):
    heads = rel_bias.shape[1]
    rows = heads * dec_seq
    tab = jnp.repeat(rel_bias.T, dec_seq, axis=0)
    return pl.pallas_call(
        functools.partial(_bias_rows_kernel, past=past, dec_seq=dec_seq),
        out_shape=jax.ShapeDtypeStruct((rows, width), F32),
        name="rel_bias_rows",
    )(tab)


def _topk_rank_rows(gm, idx, n):
    rank = jnp.zeros(gm.shape, jnp.int32)
    for m in range(n):
        row = gm[m:m + 1, :]
        beats = (row > gm) | ((row == gm) & (m < idx))
        rank = rank + beats.astype(jnp.int32)
    return rank


def _attn_prompt_kernel(q_ref, kh_ref, vt_ref, km_ref, bias_ref, o_ref, sel_ref, raw_ref, *, nblk):
    qt = pl.program_id(2)
    q_t = q_ref[0].T
    tq = q_t.shape[1]
    width = ATT_GROUP * tq
    q_all = jnp.concatenate([q_t[g * ATT_HEAD_DIM:(g + 1) * ATT_HEAD_DIM, :] for g in range(ATT_GROUP)], axis=1)
    blk = lax.broadcasted_iota(jnp.int32, (nblk, width), 0)
    past = blk < qt
    gate = _dot(km_ref[0, 0], q_all, HIGHEST)
    rank = _topk_rank_rows(jnp.where(past, gate, NEG), blk, nblk)
    sel_ref[...] = (past & (rank < MOBA_TOPK)).astype(F32)
    qs = (q_all * (ATT_HEAD_DIM ** -0.5 * LOG2E)).astype(BF16)
    key = lax.broadcasted_iota(jnp.int32, (MOBA_BLOCK, width), 0)
    qry = lax.rem(lax.broadcasted_iota(jnp.int32, (MOBA_BLOCK, width), 1), tq)

    own_tile = nblk - 1
    s = jnp.where(key <= qry, _dot(kh_ref[0, 0, qt], qs) + bias_ref[0, own_tile], NEG)
    m0 = jnp.max(s, axis=0, keepdims=True)
    carry = (m0, _dot(vt_ref[0, 0, qt], jnp.exp2(s - m0).astype(BF16)))

    def update(carry, s, vt):
        m, acc = carry
        m_new = jnp.maximum(m, jnp.max(s, axis=0, keepdims=True))
        return m_new, jnp.exp2(m - m_new) * acc + _dot(vt, jnp.exp2(s - m_new).astype(BF16))

    def single(carry):
        s = _dot(kh_ref[0, 0, 0], qs) + bias_ref[0, own_tile - qt]
        return update(carry, jnp.where(sel_ref[0:1, :] > 0.5, s, NEG), vt_ref[0, 0, 0])

    odd = lax.rem(qt, 2)
    carry = lax.cond(odd == 1, single, lambda c: c, carry)

    n_pairs = qt // 2

    def raw_scores(i):
        kb = jnp.minimum(odd + 2 * jnp.minimum(i, n_pairs - 1), nblk - 2)
        kb = jnp.maximum(kb, 0)
        return _dot(kh_ref[0, 0, pl.ds(kb, 2)].reshape(2 * MOBA_BLOCK, ATT_HEAD_DIM), qs)

    raw_ref[...] = raw_scores(0)

    def pair(i, carry):
        kb = odd + 2 * i
        s = raw_ref[...] + bias_ref[0, pl.ds(own_tile - qt + kb, 2)].reshape(2 * MOBA_BLOCK, width)
        raw_ref[...] = raw_scores(i + 1)
        s = jnp.concatenate([jnp.where(sel_ref[pl.ds(kb, 1), :] > 0.5, s[:MOBA_BLOCK], NEG),
                             jnp.where(sel_ref[pl.ds(kb + 1, 1), :] > 0.5, s[MOBA_BLOCK:], NEG)], axis=0)
        vt2 = jnp.concatenate([vt_ref[0, 0, kb], vt_ref[0, 0, kb + 1]], axis=1)
        return update(carry, s, vt2)

    _, acc = lax.fori_loop(0, n_pairs, pair, carry)
    out_t = acc[:ATT_HEAD_DIM] / acc[ATT_HEAD_DIM:ATT_HEAD_DIM + 1]
    o_ref[0] = jnp.concatenate([out_t[:, g * tq:(g + 1) * tq].T for g in range(ATT_GROUP)], axis=1)


def _attn_prompt(q, kh, vt, km, bias):
    n, t, q_w = q.shape
    kv_heads, nblk = kh.shape[1], kh.shape[2]
    assert nblk >= 2, "the look-ahead in the block-pair loop reads two key blocks"
    gw = ATT_GROUP * ATT_HEAD_DIM
    return pl.pallas_call(
        functools.partial(_attn_prompt_kernel, nblk=nblk),
        grid=(n, kv_heads, nblk),
        in_specs=[
            pl.BlockSpec((1, MOBA_BLOCK, gw), lambda i, h, j: (i, j, h)),
            pl.BlockSpec((1, 1, nblk, MOBA_BLOCK, ATT_HEAD_DIM), lambda i, h, j: (i, h, 0, 0, 0)),
            pl.BlockSpec((1, 1, nblk, VT_ROWS, MOBA_BLOCK), lambda i, h, j: (i, h, 0, 0, 0)),
            pl.BlockSpec((1, 1, nblk, ATT_HEAD_DIM), lambda i, h, j: (i, h, 0, 0)),
            pl.BlockSpec((1, nblk, MOBA_BLOCK, ATT_GROUP * MOBA_BLOCK), lambda i, h, j: (h, 0, 0, 0)),
        ],
        out_specs=pl.BlockSpec((1, MOBA_BLOCK, gw), lambda i, h, j: (i, j, h)),
        out_shape=jax.ShapeDtypeStruct((n, t, q_w), F32),
        scratch_shapes=[pltpu.VMEM((nblk, ATT_GROUP * MOBA_BLOCK), F32),
                        pltpu.VMEM((2 * MOBA_BLOCK, ATT_GROUP * MOBA_BLOCK), F32)],
        compiler_params=_cparams(("parallel", "parallel", "arbitrary")),
        name="moba_prompt",
    )(q, kh, vt, km, bias)


def _attn_sample_kernel(pt_ref, qc_ref, knew_ref, vnew_ref, bias_ref, ck_hbm, cv_hbm, o_ref,
                        kbuf, vbuf, sem, *, n_pages, dec_seq, kv_heads):
    s = pl.program_id(0)
    nseq = pl.num_programs(0)
    slot = lax.rem(s, 2)
    page = kbuf.shape[3]
    pages_per_blk = MOBA_BLOCK // page
    nblk = n_pages // pages_per_blk
    past_len = nblk * MOBA_BLOCK

    def start_fetch(seq, sl):
        def one(p, carry):
            pg = pt_ref[seq, p]
            pltpu.make_async_copy(ck_hbm.at[pg], kbuf.at[sl, p], sem.at[0, sl]).start()
            pltpu.make_async_copy(cv_hbm.at[pg], vbuf.at[sl, p], sem.at[1, sl]).start()
            return carry
        lax.fori_loop(0, n_pages, one, 0)

    @pl.when(s == 0)
    def _():
        start_fetch(0, 0)

    @pl.when(s + 1 < nseq)
    def _():
        start_fetch(s + 1, 1 - slot)

    pltpu.make_async_copy(ck_hbm.at[pl.ds(0, n_pages)], kbuf.at[slot], sem.at[0, slot]).wait()
    pltpu.make_async_copy(cv_hbm.at[pl.ds(0, n_pages)], vbuf.at[slot], sem.at[1, slot]).wait()

    qc = qc_ref[0]
    rows, kv_w = qc.shape
    scale = ATT_HEAD_DIM ** -0.5
    q_hi, q_lo = _split(qc)
    q_both = jnp.concatenate([q_hi, q_lo], axis=0)

    s_pages = []
    for p in range(n_pages):
        k_hi, k_lo = _split(kbuf[slot, p])
        t = _dot(q_both, k_hi)
        s_pages.append(t[:rows] + t[rows:] + _dot(q_hi, k_lo))

    lane = lax.broadcasted_iota(jnp.int32, (rows, LANES), 1)
    gate = jnp.zeros((rows, LANES), F32)
    for b in range(nblk):
        tot = s_pages[b * pages_per_blk]
        for p in range(b * pages_per_blk + 1, (b + 1) * pages_per_blk):
            tot = tot + s_pages[p]
        gate = jnp.where(lane == b, jnp.sum(tot, axis=1, keepdims=True) * (1.0 / MOBA_BLOCK), gate)
    rank = jnp.zeros((rows, LANES), jnp.int32)
    for m in range(nblk):
        col = gate[:, m:m + 1]
        rank = rank + ((col > gate) | ((col == gate) & (m < lane))).astype(jnp.int32)
    sel = rank < MOBA_TOPK

    masked = []
    for p in range(n_pages):
        b = p // pages_per_blk
        sp = s_pages[p] * scale + bias_ref[:, p * page:(p + 1) * page]
        masked.append(jnp.where(sel[:, b:b + 1], sp, NEG))
    n_new = knew_ref.shape[1]
    r_idx = lax.rem(lax.broadcasted_iota(jnp.int32, (rows, n_new), 0), dec_seq)
    j_idx = lax.broadcasted_iota(jnp.int32, (rows, n_new), 1)
    s_own = _dot_nt(q_hi, knew_ref[0].astype(BF16)) * scale + bias_ref[:, past_len:past_len + n_new]
    s_own = jnp.where(j_idx <= r_idx, s_own, NEG)
    m = jnp.max(s_own, axis=1, keepdims=True)
    for sp in masked:
        m = jnp.maximum(m, jnp.max(sp, axis=1, keepdims=True))
    p_own = jnp.exp(s_own - m)
    l = jnp.sum(p_own, axis=1, keepdims=True)
    acc = _dot(p_own.astype(BF16), vnew_ref[0].astype(BF16))
    for p, sp in enumerate(masked):
        prob = jnp.exp(sp - m)
        l = l + jnp.sum(prob, axis=1, keepdims=True)
        acc = acc + _dot_nt(prob.astype(BF16), vbuf[slot, p].astype(BF16))
    out = acc / l
    rpk = rows // kv_heads
    for h in range(kv_heads):
        o_ref[0, h * rpk:(h + 1) * rpk, :] = out[h * rpk:(h + 1) * rpk,
                                                 h * ATT_HEAD_DIM:(h + 1) * ATT_HEAD_DIM]


def _attn_sample(page_table, qc, k_new, v_new, bias, cache_k, cache_v, dec_seq, kv_heads):
    b, rows, kv_w = qc.shape
    n_pages = page_table.shape[1]
    page = cache_k.shape[2]
    n_new = k_new.shape[1]
    grid_spec = pltpu.PrefetchScalarGridSpec(
        num_scalar_prefetch=1,
        grid=(b,),
        in_specs=[
            pl.BlockSpec((1, rows, kv_w), lambda i, pt: (i, 0, 0)),
            pl.BlockSpec((1, n_new, kv_w), lambda i, pt: (i, 0, 0)),
            pl.BlockSpec((1, n_new, kv_w), lambda i, pt: (i, 0, 0)),
            pl.BlockSpec(bias.shape, lambda i, pt: (0, 0)),
            pl.BlockSpec(memory_space=pl.ANY),
            pl.BlockSpec(memory_space=pl.ANY),
        ],
        out_specs=pl.BlockSpec((1, rows, ATT_HEAD_DIM), lambda i, pt: (i, 0, 0)),
        scratch_shapes=[pltpu.VMEM((2, n_pages, kv_w, page), F32),
                        pltpu.VMEM((2, n_pages, kv_w, page), F32),
                        pltpu.SemaphoreType.DMA((2, 2))],
    )
    return pl.pallas_call(
        functools.partial(_attn_sample_kernel, n_pages=n_pages, dec_seq=dec_seq, kv_heads=kv_heads),
        grid_spec=grid_spec,
        out_shape=jax.ShapeDtypeStruct((b, rows, ATT_HEAD_DIM), F32),
        compiler_params=_cparams(("arbitrary",)),
        name="moba_sample",
    )(page_table, qc, k_new, v_new, bias, cache_k, cache_v)


def _gdn_gates(ba, alog, dt):
    return jax.nn.sigmoid(ba), -jnp.exp(alog) * _softplus(ba + dt)


def _gdn_prompt_kernel(x_ref, ba_ref, bat_ref, cw_ref, alog_ref, dt_ref, alogt_ref, dtt_ref,
                       o_ref, s_ref, ext_ref, *, heads):
    ct, cw = x_ref.shape[1], x_ref.shape[2]
    qk_w = heads * GDN_DK
    cs = GDN_CHUNK
    halo = SUBLANES

    @pl.when(pl.program_id(1) == 0)
    def _():
        ext_ref[0:halo, :] = jnp.zeros((halo, cw), F32)
        s_ref[...] = jnp.zeros(s_ref.shape, F32)

    x = x_ref[0]
    ext_ref[halo:halo + ct, :] = x
    w = cw_ref[...]
    conv = x * w[GDN_CONV - 1:GDN_CONV, :]
    for i in range(GDN_CONV - 1):
        conv = conv + ext_ref[pl.ds(halo - (GDN_CONV - 1) + i, ct), :] * w[i:i + 1, :]
    ext_ref[0:halo, :] = x[ct - halo:ct, :]
    act = _silu(conv)

    beta_f, g_f = _gdn_gates(ba_ref[0], alog_ref[...], dt_ref[...])
    _, g_t = _gdn_gates(bat_ref[0], alogt_ref[...], dtt_ref[...])

    r_i = lax.broadcasted_iota(jnp.int32, (cs, cs), 0)
    c_i = lax.broadcasted_iota(jnp.int32, (cs, cs), 1)
    tril = r_i >= c_i
    strict = r_i > c_i
    eye = (r_i == c_i).astype(F32)
    cum_l = tril.astype(F32)
    cum_u = (r_i <= c_i).astype(F32)

    qn, kn, vv = [], [], []
    for h in range(heads):
        qh = act[:, h * GDN_DK:(h + 1) * GDN_DK]
        kh = act[:, qk_w + h * GDN_DK:qk_w + (h + 1) * GDN_DK]
        qn.append(qh * lax.rsqrt(jnp.sum(qh * qh, axis=-1, keepdims=True) + EPS) * (GDN_DK ** -0.5))
        kn.append(kh * lax.rsqrt(jnp.sum(kh * kh, axis=-1, keepdims=True) + EPS))
        vv.append(act[:, 2 * qk_w + h * GDN_DK:2 * qk_w + (h + 1) * GDN_DK])

    n_chunks = ct // cs
    units = [(c, h) for c in range(n_chunks) for h in range(heads)]
    gc, gct = [], []
    for c in range(n_chunks):
        rows = slice(c * cs, (c + 1) * cs)
        gc.append(_dot(cum_l, g_f[rows, :], HIGHEST))
        gct.append(_dot(g_t[:, rows], cum_u, HIGHEST))
    q_u, k_u, kbeta_u, vbeta_u, gcc_u, egc_u, decay_u = {}, {}, {}, {}, {}, {}, {}
    for (c, h) in units:
        rows = slice(c * cs, (c + 1) * cs)
        beta = beta_f[rows, h:h + 1]
        gcc = gc[c][:, heads + h:heads + h + 1]
        gcr = gct[c][heads + h:heads + h + 1, :]
        q_u[c, h], k_u[c, h] = qn[h][rows], kn[h][rows]
        kbeta_u[c, h] = k_u[c, h] * beta
        vbeta_u[c, h] = vv[h][rows] * beta
        gcc_u[c, h], egc_u[c, h] = gcc, jnp.exp(gcc)
        decay_u[c, h] = jnp.where(tril, jnp.exp(jnp.where(tril, gcc - gcr, 0.0)), 0.0)
    a_u = {u: _mm3_nt(jnp.concatenate([kbeta_u[u], q_u[u]], axis=0), k_u[u]) for u in units}
    x_u = {u: -jnp.where(strict, a_u[u][:cs] * decay_u[u], 0.0) for u in units}
    intra_u = {u: jnp.where(tril, a_u[u][cs:] * decay_u[u], 0.0) for u in units}
    p_u = {u: eye + x_u[u] for u in units}
    xp_u = {u: _mm3(x_u[u], x_u[u]) for u in units}
    npow = 2
    while npow * 2 < cs:
        pr_u = {u: _mm3(jnp.concatenate([p_u[u], xp_u[u]], axis=0), xp_u[u]) for u in units}
        p_u = {u: p_u[u] + pr_u[u][:cs] for u in units}
        xp_u = {u: pr_u[u][cs:] for u in units}
        npow *= 2
    p_u = {u: p_u[u] + _mm3(p_u[u], xp_u[u]) for u in units}
    uw_u = {u: _dot(p_u[u].astype(BF16),
                    jnp.concatenate([vbeta_u[u], kbeta_u[u] * egc_u[u]], axis=1).astype(BF16)) for u in units}

    state = [s_ref[0, h] for h in range(heads)]
    for c in range(n_chunks):
        rows = slice(c * cs, (c + 1) * cs)
        ws = [_dot(jnp.concatenate([uw_u[c, h][:, GDN_DK:], q_u[c, h] * egc_u[c, h]], axis=0).astype(BF16),
                   state[h].astype(BF16)) for h in range(heads)]
        v_new = [uw_u[c, h][:, :GDN_DK] - ws[h][:cs] for h in range(heads)]
        v_bf = [v.astype(BF16) for v in v_new]
        for h in range(heads):
            o_ref[0, rows, h * GDN_DK:(h + 1) * GDN_DK] = ws[h][cs:] + _dot(intra_u[c, h].astype(BF16), v_bf[h])
        for h in range(heads):
            gcc = gcc_u[c, h]
            g_last = gcc[cs - 1:cs, :]
            state[h] = state[h] * jnp.exp(g_last) + _dot_tn(
                (k_u[c, h] * jnp.exp(g_last - gcc)).astype(BF16), v_bf[h])
    for h in range(heads):
        s_ref[0, h] = state[h]


def _gdn_prompt(conv_pre, ba, conv_w, a_log, dt_bias, heads, ct):
    n, t, cw = conv_pre.shape
    bat = jnp.swapaxes(ba[:, :, :2 * heads], 1, 2)
    pad = jnp.zeros((heads,), F32)
    alog = jnp.concatenate([pad, a_log])
    dt = jnp.concatenate([pad, dt_bias])
    lane_pad = (0, LANES - 2 * heads)
    return pl.pallas_call(
        functools.partial(_gdn_prompt_kernel, heads=heads),
        grid=(n, t // ct),
        in_specs=[
            pl.BlockSpec((1, ct, cw), lambda i, j: (i, j, 0)),
            pl.BlockSpec((1, ct, LANES), lambda i, j: (i, j, 0)),
            pl.BlockSpec((1, 2 * heads, ct), lambda i, j: (i, 0, j)),
            pl.BlockSpec(conv_w.shape, lambda i, j: (0, 0)),
            pl.BlockSpec((1, LANES), lambda i, j: (0, 0)),
            pl.BlockSpec((1, LANES), lambda i, j: (0, 0)),
            pl.BlockSpec((2 * heads, 1), lambda i, j: (0, 0)),
            pl.BlockSpec((2 * heads, 1), lambda i, j: (0, 0)),
        ],
        out_specs=[pl.BlockSpec((1, ct, heads * GDN_DK), lambda i, j: (i, j, 0)),
                   pl.BlockSpec((1, heads, GDN_DK, GDN_DK), lambda i, j: (i, 0, 0, 0))],
        out_shape=[jax.ShapeDtypeStruct((n, t, heads * GDN_DK), F32),
                   jax.ShapeDtypeStruct((n, heads, GDN_DK, GDN_DK), F32)],
        scratch_shapes=[pltpu.VMEM((SUBLANES + ct, cw), F32)],
        compiler_params=_cparams(("parallel", "arbitrary")),
        name="gdn_prompt",
    )(conv_pre, ba, bat, conv_w, jnp.pad(alog, lane_pad).reshape(1, LANES),
      jnp.pad(dt, lane_pad).reshape(1, LANES), alog.reshape(2 * heads, 1), dt.reshape(2 * heads, 1))


def _gdn_sample_kernel(ext_ref, ba_ref, cw_ref, alog_ref, dt_ref, s_in_ref, o_ref, s_out_ref, *, heads):
    t_len = ba_ref.shape[0]
    gs = ext_ref.shape[1]
    qk_w = heads * GDN_DK
    w = cw_ref[...]
    q_t, k_t, v_r, beta_r, dec_r = [], [], [], [], []
    for t in range(t_len):
        conv = ext_ref[t] * w[0:1, :]
        for i in range(1, GDN_CONV):
            conv = conv + ext_ref[t + i] * w[i:i + 1, :]
        act = _silu(conv)
        beta_f, g_f = _gdn_gates(ba_ref[t], alog_ref[...], dt_ref[...])
        beta_r.append(beta_f)
        dec_r.append(jnp.exp(g_f))
        qs, ks, vs = [], [], []
        for h in range(heads):
            qh = act[:, h * GDN_DK:(h + 1) * GDN_DK]
            kh = act[:, qk_w + h * GDN_DK:qk_w + (h + 1) * GDN_DK]
            qn = qh * lax.rsqrt(jnp.sum(qh * qh, axis=-1, keepdims=True) + EPS) * (GDN_DK ** -0.5)
            kn = kh * lax.rsqrt(jnp.sum(kh * kh, axis=-1, keepdims=True) + EPS)
            qs.append(qn.T)
            ks.append(kn.T)
            vs.append(act[:, 2 * qk_w + h * GDN_DK:2 * qk_w + (h + 1) * GDN_DK])
        q_t.append(qs)
        k_t.append(ks)
        v_r.append(vs)
    for i in range(gs):
        for h in range(heads):
            st = s_in_ref[i, h]
            for t in range(t_len):
                st = st * dec_r[t][i:i + 1, heads + h:heads + h + 1]
                kcol = k_t[t][h][:, i:i + 1]
                ks_row = jnp.sum(st * kcol, axis=0, keepdims=True)
                delta = (v_r[t][h][i:i + 1, :] - ks_row) * beta_r[t][i:i + 1, h:h + 1]
                st = st + kcol * delta
                o_ref[t, i:i + 1, h * GDN_DK:(h + 1) * GDN_DK] = jnp.sum(
                    st * q_t[t][h][:, i:i + 1], axis=0, keepdims=True)
            s_out_ref[i, h] = st


def _gdn_sample(ext_tm, ba_tm, conv_w, a_log, dt_bias, state, heads, gs):
    t_len, b, _ = ba_tm.shape
    cw = ext_tm.shape[2]
    pad = jnp.zeros((heads,), F32)
    lane_pad = (0, LANES - 2 * heads)
    alog = jnp.pad(jnp.concatenate([pad, a_log]), lane_pad).reshape(1, LANES)
    dt = jnp.pad(jnp.concatenate([pad, dt_bias]), lane_pad).reshape(1, LANES)
    return pl.pallas_call(
        functools.partial(_gdn_sample_kernel, heads=heads),
        grid=(b // gs,),
        in_specs=[
            pl.BlockSpec((ext_tm.shape[0], gs, cw), lambda i: (0, i, 0)),
            pl.BlockSpec((t_len, gs, LANES), lambda i: (0, i, 0)),
            pl.BlockSpec(conv_w.shape, lambda i: (0, 0)),
            pl.BlockSpec((1, LANES), lambda i: (0, 0)),
            pl.BlockSpec((1, LANES), lambda i: (0, 0)),
            pl.BlockSpec((gs, heads, GDN_DK, GDN_DK), lambda i: (i, 0, 0, 0)),
        ],
        out_specs=[pl.BlockSpec((t_len, gs, heads * GDN_DK), lambda i: (0, i, 0)),
                   pl.BlockSpec((gs, heads, GDN_DK, GDN_DK), lambda i: (i, 0, 0, 0))],
        out_shape=[jax.ShapeDtypeStruct((t_len, b, heads * GDN_DK), F32),
                   jax.ShapeDtypeStruct(state.shape, F32)],
        compiler_params=_cparams(("parallel",)),
        name="gdn_sample",
    )(ext_tm, ba_tm, conv_w, alog, dt, state)


def _mid_kernel(x_ref, attn_ref, og_ref, z_ref, gta_ref, scf_ref, shf_ref, gw_ref, wout_ref, nf_ref,
                wr_ref, br_ref, x1_ref, h2_ref, gates_ref, topi_ref=None, *, heads, grouped):
    og = og_ref[0]
    z = z_ref[0]
    parts = [attn_ref[0]]
    for h in range(heads):
        o = og[:, h * GDN_DK:(h + 1) * GDN_DK]
        y = o * lax.rsqrt(jnp.mean(o * o, axis=-1, keepdims=True) + EPS) * gw_ref[...]
        parts.append(y * _silu(z[:, h * GDN_DK:(h + 1) * GDN_DK]))
    mix = jnp.concatenate(parts, axis=1).astype(BF16)
    x1 = x_ref[0] + gta_ref[0] * _dot(mix, wout_ref[...])
    x1_ref[0] = x1
    h2 = (x1 * lax.rsqrt(jnp.mean(x1 * x1, axis=-1, keepdims=True) + EPS) * nf_ref[...]
          * (1.0 + scf_ref[0]) + shf_ref[0])
    logits = _dot(h2, wr_ref[...], HIGHEST) + br_ref[...]
    lane = lax.broadcasted_iota(jnp.int32, logits.shape, 1)
    work = logits
    sel = jnp.zeros(logits.shape, jnp.bool_)
    vals, idxs = [], []
    for j in range(TOP_K):
        mx = jnp.max(work, axis=-1, keepdims=True)
        idx = jnp.min(jnp.where(work == mx, lane, LANES), axis=-1, keepdims=True)
        pick = lane == idx
        vals.append(mx)
        idxs.append(idx)
        sel = sel | pick
        work = jnp.where(pick, -jnp.inf, work)
    if grouped:
        tt = h2.shape[0]
        for s in range(h2.shape[1] // LANES):
            h2_ref[0, pl.ds(s, tt, stride=h2.shape[1] // LANES), :] = h2[:, s * LANES:(s + 1) * LANES]
        ex = [jnp.exp(v - vals[0]) for v in vals]
        denom = ex[0]
        for v in ex[1:]:
            denom = denom + v
        topi = jnp.zeros(logits.shape, jnp.int32)
        topw = jnp.zeros(logits.shape, F32)
        for j in range(TOP_K):
            topi = jnp.where(lane == j, idxs[j], topi)
            topw = jnp.where(lane == j, ex[j] / denom, topw)
        gates_ref[0] = topw
        topi_ref[0] = topi
    else:
        h2_ref[0] = h2.astype(BF16)
        e = jnp.where(sel, jnp.exp(logits - vals[0]), 0.0)
        gates_ref[0] = e / jnp.sum(e, axis=-1, keepdims=True)


def _mid(x, attn, og, z, gta, scf, shf, gdn_norm_w, w_out_bf, norm_ffn_w, wr_pad, br_pad, tt, heads, grouped):
    n, t, d = x.shape
    slab = d // LANES
    per_token = gta.shape[1] != 1
    mod_spec = (pl.BlockSpec((1, tt, d), lambda i, j: (i, j, 0)) if per_token
                else pl.BlockSpec((1, 1, d), lambda i, j: (i, 0, 0)))

    def tok(width):
        return pl.BlockSpec((1, tt, width), lambda i, j: (i, j, 0))

    def full(a):
        return pl.BlockSpec(a.shape, lambda i, j: (0,) * a.ndim)

    gw = gdn_norm_w.reshape(1, -1)
    nf = norm_ffn_w.reshape(1, d)
    if grouped:
        out_specs = [tok(d), pl.BlockSpec((1, tt * slab, LANES), lambda i, j: (i, j, 0)), tok(LANES), tok(LANES)]
        out_shape = [jax.ShapeDtypeStruct((n, t, d), F32), jax.ShapeDtypeStruct((n, t * slab, LANES), F32),
                     jax.ShapeDtypeStruct((n, t, LANES), F32), jax.ShapeDtypeStruct((n, t, LANES), jnp.int32)]
    else:
        out_specs = [tok(d), tok(d), tok(LANES)]
        out_shape = [jax.ShapeDtypeStruct((n, t, d), F32), jax.ShapeDtypeStruct((n, t, d), BF16),
                     jax.ShapeDtypeStruct((n, t, LANES), F32)]
    return pl.pallas_call(
        functools.partial(_mid_kernel, heads=heads, grouped=grouped),
        grid=(n, t // tt),
        in_specs=[tok(d), tok(attn.shape[2]), tok(og.shape[2]), tok(z.shape[2]), mod_spec, mod_spec, mod_spec,
                  full(gw), full(w_out_bf), full(nf), full(wr_pad), full(br_pad)],
        out_specs=out_specs,
        out_shape=out_shape,
        compiler_params=_cparams(("parallel", "parallel")),
        name="out_proj_router_grouped" if grouped else "out_proj_router",
    )(x, attn, og, z, gta, scf, shf, gw, w_out_bf, nf, wr_pad, br_pad)


def _expert_ffn(h, wup_ref, bup_ref, wdn_ref, bdn_ref, ff_chunk):
    d_ff = wdn_ref.shape[1]
    y = jnp.zeros((h.shape[0], wdn_ref.shape[2]), F32)
    for c in range(d_ff // ff_chunk):
        lo, hi = c * ff_chunk, (c + 1) * ff_chunk
        x_glu = _dot(h, wup_ref[0, :, lo:hi]) + bup_ref[0, :, lo:hi]
        x_lin = _dot(h, wup_ref[0, :, d_ff + lo:d_ff + hi]) + bup_ref[0, :, d_ff + lo:d_ff + hi]
        x_glu = jnp.minimum(x_glu, SWIGLU_LIMIT)
        x_lin = jnp.clip(x_lin, -SWIGLU_LIMIT, SWIGLU_LIMIT)
        act = x_glu * jax.nn.sigmoid(SWIGLU_ALPHA * x_glu) * (x_lin + 1.0)
        y = y + _dot(act.astype(BF16), wdn_ref[0, lo:hi, :])
    return y + bdn_ref[0]


def _moe_kernel(h2_ref, gates_ref, x1_ref, gtf_ref, wup_ref, bup_ref, wdn_ref, bdn_ref, nw_ref,
                y_ref, acc_ref, *, ff_chunk):
    e = pl.program_id(2)
    n_exp = pl.num_programs(2)

    @pl.when(e == 0)
    def _():
        acc_ref[...] = jnp.zeros(acc_ref.shape, F32)

    gates = gates_ref[0]
    lane = lax.broadcasted_iota(jnp.int32, gates.shape, 1)
    gcol = jnp.sum(jnp.where(lane == e, gates, 0.0), axis=-1, keepdims=True)
    acc_ref[...] += gcol * _expert_ffn(h2_ref[0], wup_ref, bup_ref, wdn_ref, bdn_ref, ff_chunk)

    @pl.when(e == n_exp - 1)
    def _():
        x = x1_ref[0] + gtf_ref[0] * acc_ref[...]
        y_ref[0] = x * lax.rsqrt(jnp.mean(x * x, axis=-1, keepdims=True) + EPS) * nw_ref[...]


def _moe(h2, gates, x1, gtf, w_up_bf, b_up, w_dn_bf, b_dn, norm_final_w, tm):
    n, t, d = x1.shape
    n_exp, _, ff2 = w_up_bf.shape
    d_ff = ff2 // 2
    per_token = gtf.shape[1] != 1
    mod_spec = (pl.BlockSpec((1, tm, d), lambda i, j, e: (i, j, 0)) if per_token
                else pl.BlockSpec((1, 1, d), lambda i, j, e: (i, 0, 0)))

    def tok(width):
        return pl.BlockSpec((1, tm, width), lambda i, j, e: (i, j, 0))

    return pl.pallas_call(
        functools.partial(_moe_kernel, ff_chunk=min(512, d_ff)),
        grid=(n, t // tm, n_exp),
        in_specs=[tok(d), tok(LANES), tok(d), mod_spec,
                  pl.BlockSpec((1, d, ff2), lambda i, j, e: (e, 0, 0)),
                  pl.BlockSpec((1, 1, ff2), lambda i, j, e: (e, 0, 0)),
                  pl.BlockSpec((1, d_ff, d), lambda i, j, e: (e, 0, 0)),
                  pl.BlockSpec((1, 1, d), lambda i, j, e: (e, 0, 0)),
                  pl.BlockSpec((1, d), lambda i, j, e: (0, 0))],
        out_specs=tok(d),
        out_shape=jax.ShapeDtypeStruct((n, t, d), F32),
        scratch_shapes=[pltpu.VMEM((tm, d), F32)],
        compiler_params=_cparams(("parallel", "parallel", "arbitrary")),
        name="moe_ffn",
    )(h2, gates, x1, gtf, w_up_bf, b_up.reshape(n_exp, 1, ff2), w_dn_bf, b_dn.reshape(n_exp, 1, d),
      norm_final_w.reshape(1, d))


MOE_GROUP = 4096
MOE_CHUNK = 576
MOE_SCATTER_BATCH = 8


def _moe_grouped_kernel(cnt_ref, off_ref, tok_ref, wgt_ref, xg_ref, wup_ref, bup_ref, wdn_ref, bdn_ref,
                        y_ref, xbuf, obuf, *, group, chunk, ff_chunk):
    g = pl.program_id(0)
    e = pl.program_id(1)
    slab = xbuf.shape[0] // chunk

    @pl.when(e == 0)
    def _():
        y_ref[...] = jnp.zeros(y_ref.shape, F32)

    seg = g * pl.num_programs(1) + e
    off = off_ref[seg]
    end = off + cnt_ref[seg]

    def one_chunk(j, carry):
        r0 = off + j * chunk
        for i in range(chunk):
            tok = tok_ref[0, 0, r0 + i]
            xbuf[pl.ds(i * slab, slab), :] = xg_ref[0, pl.ds(pl.multiple_of(tok * slab, slab), slab), :]
        x = jnp.concatenate([xbuf[pl.ds(s, chunk, stride=slab), :] for s in range(slab)], axis=1)
        y = _expert_ffn(x.astype(BF16), wup_ref, bup_ref, wdn_ref, bdn_ref, ff_chunk)
        for s in range(slab):
            obuf[pl.ds(s, chunk, stride=slab), :] = y[:, s * LANES:(s + 1) * LANES]
        for b0 in range(0, chunk, MOE_SCATTER_BATCH):
            rows, wgts = [], []
            for i in range(b0, b0 + MOE_SCATTER_BATCH):
                valid = r0 + i < end
                tok = jnp.where(valid, tok_ref[0, 0, r0 + i], group)
                rows.append(pl.multiple_of(tok * slab, slab))
                wgts.append(jnp.where(valid, wgt_ref[0, 0, r0 + i], 0.0))
            cur = [y_ref[0, pl.ds(r, slab), :] for r in rows]
            new = [cur[k] + wgts[k] * obuf[pl.ds((b0 + k) * slab, slab), :] for k in range(MOE_SCATTER_BATCH)]
            for k in range(MOE_SCATTER_BATCH):
                y_ref[0, pl.ds(rows[k], slab), :] = new[k]
        return carry

    lax.fori_loop(0, (end - off + chunk - 1) // chunk, one_chunk, 0)


def _moe_grouped(h2_slabs, topw, topi, w_up_bf, b_up, w_dn_bf, b_dn, group):
    m = topw.shape[0]
    slab = h2_slabs.shape[0] // m
    n_exp, d, ff2 = w_up_bf.shape
    d_ff = ff2 // 2
    ngroups = m // group
    rows = group * TOP_K
    eid = topi[:, :TOP_K].reshape(ngroups, rows)
    order = jnp.argsort(eid, axis=1, stable=True)
    tok_sorted = (order // TOP_K).astype(jnp.int32)
    wgt_sorted = jnp.take_along_axis(topw[:, :TOP_K].reshape(ngroups, rows), order, axis=1)
    counts = jnp.sum((eid[:, :, None] == jnp.arange(n_exp, dtype=jnp.int32)).astype(jnp.int32), axis=1)
    offs = jnp.cumsum(counts, axis=1) - counts
    tail = ((0, 0), (0, MOE_CHUNK))
    tok_sorted = jnp.pad(tok_sorted, tail).reshape(ngroups, 1, rows + MOE_CHUNK)
    wgt_sorted = jnp.pad(wgt_sorted, tail).reshape(ngroups, 1, rows + MOE_CHUNK)

    grid_spec = pltpu.PrefetchScalarGridSpec(
        num_scalar_prefetch=2,
        grid=(ngroups, n_exp),
        in_specs=[
            pl.BlockSpec((1, 1, rows + MOE_CHUNK), lambda g, e, c, o: (g, 0, 0), memory_space=pltpu.SMEM),
            pl.BlockSpec((1, 1, rows + MOE_CHUNK), lambda g, e, c, o: (g, 0, 0), memory_space=pltpu.SMEM),
            pl.BlockSpec((1, group * slab, LANES), lambda g, e, c, o: (g, 0, 0), pipeline_mode=pl.Buffered(1)),
            pl.BlockSpec((1, d, ff2), lambda g, e, c, o: (e, 0, 0)),
            pl.BlockSpec((1, 1, ff2), lambda g, e, c, o: (e, 0, 0)),
            pl.BlockSpec((1, d_ff, d), lambda g, e, c, o: (e, 0, 0)),
            pl.BlockSpec((1, 1, d), lambda g, e, c, o: (e, 0, 0)),
        ],
        out_specs=pl.BlockSpec((1, (group + 1) * slab, LANES), lambda g, e, c, o: (g, 0, 0),
                               pipeline_mode=pl.Buffered(1)),
        scratch_shapes=[pltpu.VMEM((MOE_CHUNK * slab, LANES), F32), pltpu.VMEM((MOE_CHUNK * slab, LANES), F32)],
    )
    return pl.pallas_call(
        functools.partial(_moe_grouped_kernel, group=group, chunk=MOE_CHUNK, ff_chunk=min(512, d_ff)),
        grid_spec=grid_spec,
        out_shape=jax.ShapeDtypeStruct((ngroups, (group + 1) * slab, LANES), F32),
        compiler_params=pltpu.CompilerParams(dimension_semantics=("parallel", "arbitrary"),
                                             vmem_limit_bytes=MOE_VMEM_LIMIT),
        name="moe_grouped",
    )(counts.reshape(-1), offs.reshape(-1), tok_sorted, wgt_sorted,
      h2_slabs.reshape(ngroups, group * slab, LANES), w_up_bf, b_up.reshape(n_exp, 1, ff2), w_dn_bf,
      b_dn.reshape(n_exp, 1, d))


def _moe_finish_kernel(moe_ref, x1_ref, gtf_ref, nw_ref, y_ref):
    tt, d = x1_ref.shape[1], x1_ref.shape[2]
    slab = d // LANES
    moe = jnp.concatenate([moe_ref[0, pl.ds(s, tt, stride=slab), :] for s in range(slab)], axis=1)
    x = x1_ref[0] + gtf_ref[0] * moe
    y_ref[0] = x * lax.rsqrt(jnp.mean(x * x, axis=-1, keepdims=True) + EPS) * nw_ref[...]


def _moe_finish(moe_slabs, x1, gtf, norm_final_w, group, tt):
    n, t, d = x1.shape
    slab = d // LANES
    per_group = group // tt
    steps = t // tt
    return pl.pallas_call(
        _moe_finish_kernel,
        grid=(n, steps),
        in_specs=[
            pl.BlockSpec((1, tt * slab, LANES),
                         lambda i, j: ((i * steps + j) // per_group, (i * steps + j) % per_group, 0)),
            pl.BlockSpec((1, tt, d), lambda i, j: (i, j, 0)),
            pl.BlockSpec((1, 1, d), lambda i, j: (i, 0, 0)),
            pl.BlockSpec((1, d), lambda i, j: (0, 0)),
        ],
        out_specs=pl.BlockSpec((1, tt, d), lambda i, j: (i, j, 0)),
        out_shape=jax.ShapeDtypeStruct((n, t, d), F32),
        compiler_params=_cparams(("parallel", "parallel")),
        name="moe_finish",
    )(moe_slabs, x1, gtf, norm_final_w.reshape(1, d))


def _pick_tile(t, pref):
    tile = min(t, pref)
    assert t % tile == 0, (t, tile)
    return tile


def kernel(x_prompt, x_sample, cache_k, cache_v, state_conv, state_ssm, page_table, c_prompt, c_sample, w_ada, b_ada, norm_attn_w, norm_ffn_w, norm_final_w, w_in, rel_bias, conv_w, a_log, dt_bias, gdn_norm_w, w_out, w_router, b_router, w_up, b_up, w_down, b_down):
    depth = w_in.shape[0]
    assert depth == 1, "single-layer step"
    nb, seq, d = x_prompt.shape
    db, dec_seq, _ = x_sample.shape
    n_pool, page, kv_heads, dh = cache_k.shape[1:]
    assert dh == ATT_HEAD_DIM
    att_heads = rel_bias.shape[1]
    assert att_heads == kv_heads * ATT_GROUP
    gdn_heads = a_log.shape[1]
    n_pages = page_table.shape[1]
    past = n_pages * page
    q_w = att_heads * dh
    kv_w = kv_heads * dh
    gdn_cw = conv_w.shape[2]
    z_w = gdn_heads * GDN_DK
    assert gdn_cw == 3 * z_w
    in_w = q_w + 2 * kv_w + gdn_cw + z_w + 2 * gdn_heads
    assert w_in.shape[2] == in_w and in_w - 2 * gdn_heads == (in_w // LANES) * LANES
    assert seq % MOBA_BLOCK == 0 and past % MOBA_BLOCK == 0 and MOBA_BLOCK % page == 0
    assert dec_seq <= MOBA_BLOCK and dec_seq >= GDN_CONV - 1 and dec_seq <= SUBLANES
    n_exp = w_router.shape[2]
    nblk = seq // MOBA_BLOCK
    widths = (q_w, kv_w, gdn_cw, z_w)

    n_mod = nb + db
    n_mod_pad = -(-n_mod // SUBLANES) * SUBLANES
    c_all = jnp.pad(jnp.concatenate([c_prompt, c_sample], axis=0), ((0, n_mod_pad - n_mod), (0, 0)))
    mod = _ada(c_all, w_ada[0], b_ada[0])
    mod_p = [mod[:nb, i * d:(i + 1) * d].reshape(nb, 1, d) for i in range(6)]
    m_s = db * dec_seq
    mod_s = [jnp.broadcast_to(mod[nb:n_mod, None, i * d:(i + 1) * d], (db, dec_seq, d)).reshape(1, m_s, d)
             for i in range(6)]

    w_in_pad = jnp.pad(w_in[0], ((0, 0), (0, LANES - 2 * gdn_heads))).astype(BF16)
    w_out_bf = w_out[0].astype(BF16)
    wr_pad = jnp.pad(w_router[0], ((0, 0), (0, LANES - n_exp)))
    br_pad = jnp.pad(b_router[0], (0, LANES - n_exp), constant_values=NEG).reshape(1, LANES)
    w_up_bf = w_up[0].astype(BF16)
    w_dn_bf = w_down[0].astype(BF16)

    tt_p = _pick_tile(seq, 512)
    q_p, k_p, v_p, conv_p, z_p, ba_p, kh_p, vt_p, km_p = _inproj(
        x_prompt, mod_p[0], mod_p[1], norm_attn_w[0], w_in_pad, widths, tt_p, True)
    km_h = km_p.reshape(nb, nblk, kv_heads, dh).transpose(0, 2, 1, 3)
    bias_t = _bias_tiles(rel_bias, nblk)
    attn_p = _attn_prompt(q_p, kh_p, vt_p, km_h, bias_t)
    og_p, ssm_p = _gdn_prompt(conv_p, ba_p, conv_w[0], a_log[0], dt_bias[0], gdn_heads, _pick_tile(seq, 256))
    x1_p, h2_p, topw_p, topi_p = _mid(x_prompt, attn_p, og_p, z_p, mod_p[2], mod_p[4], mod_p[3], gdn_norm_w[0],
                                      w_out_bf, norm_ffn_w[0], wr_pad, br_pad, tt_p, gdn_heads, True)
    m_p = nb * seq
    group = _pick_tile(m_p, MOE_GROUP)
    assert group % tt_p == 0 and seq % tt_p == 0
    moe_p = _moe_grouped(h2_p.reshape(m_p * (d // LANES), LANES), topw_p.reshape(m_p, LANES),
                         topi_p.reshape(m_p, LANES), w_up_bf, b_up[0], w_dn_bf, b_down[0], group)
    y_p = _moe_finish(moe_p, x1_p, mod_p[5], norm_final_w, group, tt_p)

    tt_s = _pick_tile(m_s, 512)
    q_s, k_s, v_s, conv_s, z_s, ba_s = _inproj(
        x_sample.reshape(1, m_s, d), mod_s[0], mod_s[1], norm_attn_w[0], w_in_pad, widths, tt_s, False)
    q4 = q_s.reshape(db, dec_seq, kv_heads, ATT_GROUP, dh).transpose(0, 2, 3, 1, 4)
    q4 = q4.reshape(db, kv_heads, ATT_GROUP * dec_seq, 1, dh)
    eye = jnp.eye(kv_heads, dtype=F32).reshape(1, kv_heads, 1, kv_heads, 1)
    qc = (q4 * eye).reshape(db, att_heads * dec_seq, kv_w)
    new_pad = ((0, 0), (0, SUBLANES - dec_seq), (0, 0))
    k_new = jnp.pad(k_s.reshape(db, dec_seq, kv_w), new_pad)
    v_new = jnp.pad(v_s.reshape(db, dec_seq, kv_w), new_pad)
    bias_r = _bias_rows(rel_bias, past, dec_seq, past + MOBA_BLOCK)
    ck_t = cache_k[0].transpose(0, 2, 3, 1).reshape(n_pool, kv_w, page)
    cv_t = cache_v[0].transpose(0, 2, 3, 1).reshape(n_pool, kv_w, page)
    attn_rows = _attn_sample(page_table, qc, k_new, v_new, bias_r, ck_t, cv_t, dec_seq, kv_heads)
    attn_s = attn_rows.reshape(db, att_heads, dec_seq, dh).transpose(0, 2, 1, 3).reshape(1, m_s, q_w)

    conv_s3 = conv_s.reshape(db, dec_seq, gdn_cw)
    ext = jnp.concatenate([state_conv[0], conv_s3], axis=1)
    gs = _pick_tile(db, SUBLANES)
    og_tm, ssm_s = _gdn_sample(jnp.swapaxes(ext, 0, 1), jnp.swapaxes(ba_s.reshape(db, dec_seq, LANES), 0, 1),
                               conv_w[0], a_log[0], dt_bias[0], state_ssm[0], gdn_heads, gs)
    og_s = jnp.swapaxes(og_tm, 0, 1).reshape(1, m_s, z_w)
    x1_s, h2_s, gates_s = _mid(x_sample.reshape(1, m_s, d), attn_s, og_s, z_s, mod_s[2], mod_s[4], mod_s[3],
                               gdn_norm_w[0], w_out_bf, norm_ffn_w[0], wr_pad, br_pad, tt_s, gdn_heads, False)
    y_s = _moe(h2_s, gates_s, x1_s, mod_s[5], w_up_bf, b_up[0], w_dn_bf, b_down[0], norm_final_w, tt_s)

    return (
        y_p,
        y_s.reshape(db, dec_seq, d),
        k_p.reshape(1, nb, seq, kv_heads, dh),
        v_p.reshape(1, nb, seq, kv_heads, dh),
        conv_p[:, seq - (GDN_CONV - 1):, :][None],
        ssm_p[None],
        k_s.reshape(1, db, dec_seq, kv_heads, dh),
        v_s.reshape(1, db, dec_seq, kv_heads, dh),
        ext[:, dec_seq:, :][None],
        ssm_s[None],
    )
```

```python
import functools
import math

import jax
import jax.numpy as jnp
from jax import lax
from jax.experimental import pallas as pl
from jax.experimental.pallas import tpu as pltpu

F32 = jnp.float32
BF16 = jnp.bfloat16
HIGHEST = lax.Precision.HIGHEST

ATT_HEAD_DIM = 64
ATT_GROUP = 2
MOBA_BLOCK = 256
MOBA_TOPK = 3
REL_BUCKETS = 32
REL_MAX_DIST = 4096
GDN_DK = 128
GDN_CONV = 4
GDN_CHUNK = 64
TOP_K = 4
SWIGLU_LIMIT = 7.0
SWIGLU_ALPHA = 1.702
EPS = 1e-6
NEG = -1e30

LANES = 128
SUBLANES = 8
VT_ROWS = ATT_HEAD_DIM + SUBLANES
LOG2E = math.log2(math.e)
VMEM_LIMIT = 48 * 1024 * 1024
MOE_VMEM_LIMIT = 56 * 1024 * 1024


def _cparams(sem):
    return pltpu.CompilerParams(dimension_semantics=sem, vmem_limit_bytes=VMEM_LIMIT)


def _dot(a, b, precision=None):
    return jnp.dot(a, b, precision=precision, preferred_element_type=F32)


def _dot_nt(a, b, precision=None):
    return lax.dot_general(a, b, (((1,), (1,)), ((), ())), precision=precision,
                           preferred_element_type=F32)


def _dot_tn(a, b, precision=None):
    return lax.dot_general(a, b, (((0,), (0,)), ((), ())), precision=precision,
                           preferred_element_type=F32)


def _split(x):
    hi = x.astype(BF16)
    return hi, (x - hi.astype(F32)).astype(BF16)


def _mm3(a, b):
    m = a.shape[0]
    ah, al = _split(a)
    bh, bl = _split(b)
    t = _dot(jnp.concatenate([ah, al], axis=0), bh)
    return t[:m] + t[m:] + _dot(ah, bl)


def _mm3_nt(a, b):
    m = a.shape[0]
    ah, al = _split(a)
    bh, bl = _split(b)
    t = _dot_nt(jnp.concatenate([ah, al], axis=0), bh)
    return t[:m] + t[m:] + _dot_nt(ah, bl)


def _silu(x):
    return x * jax.nn.sigmoid(x)


def _softplus(x):
    return jnp.maximum(x, 0.0) + jnp.log1p(jnp.exp(-jnp.abs(x)))


def _rel_bucket(dist):
    n = jnp.maximum(dist, 0)
    max_exact = REL_BUCKETS // 2
    nf = jnp.maximum(n, max_exact).astype(F32)
    large = max_exact + (jnp.log(nf / max_exact) / math.log(REL_MAX_DIST / max_exact)
                         * (REL_BUCKETS - max_exact)).astype(jnp.int32)
    return jnp.where(n < max_exact, n, jnp.minimum(large, REL_BUCKETS - 1))


def _ada_kernel(c_ref, w_ref, b_ref, o_ref):
    o_ref[...] = _dot(_silu(c_ref[...]), w_ref[...], HIGHEST) + b_ref[...]


def _ada(c, w_ada, b_ada):
    m, d = c.shape
    n = w_ada.shape[1]
    tn = d
    return pl.pallas_call(
        _ada_kernel,
        grid=(n // tn,),
        in_specs=[pl.BlockSpec((m, d), lambda j: (0, 0)),
                  pl.BlockSpec((d, tn), lambda j: (0, j)),
                  pl.BlockSpec((1, tn), lambda j: (0, j))],
        out_specs=pl.BlockSpec((m, tn), lambda j: (0, j)),
        out_shape=jax.ShapeDtypeStruct((m, n), F32),
        compiler_params=_cparams(("parallel",)),
        name="ada_modulation",
    )(c, w_ada, b_ada.reshape(1, n))


def _inproj_kernel(x_ref, sh_ref, sc_ref, nw_ref, w_ref, *out_refs, offs, heads_out, kv_heads):
    x = x_ref[0]
    xn = x * lax.rsqrt(jnp.mean(x * x, axis=-1, keepdims=True) + EPS) * nw_ref[...]
    h = (xn * (1.0 + sc_ref[0]) + sh_ref[0]).astype(BF16)
    o0, o1, o2, o3, o4, o5 = offs

    def proj(lo, hi):
        return _dot(h, w_ref[:, lo:hi])

    q_ref, k_ref, v_ref, conv_ref, z_ref, ba_ref = out_refs[:6]
    q_ref[0] = proj(o0, o1)
    k = proj(o1, o2)
    v = proj(o2, o3)
    k_ref[0] = k
    v_ref[0] = v
    conv_ref[0] = proj(o3, o4)
    z_ref[0] = proj(o4, o5)
    ba_ref[0] = proj(o5, o5 + LANES)
    if heads_out:
        kh_ref, vt_ref, km_ref = out_refs[6:]
        vt = v.T
        for b in range(x.shape[0] // MOBA_BLOCK):
            rows = slice(b * MOBA_BLOCK, (b + 1) * MOBA_BLOCK)
            kb = k[rows]
            km_ref[0, b] = jnp.mean(kb, axis=0, keepdims=True)
            for hh in range(kv_heads):
                cols = slice(hh * ATT_HEAD_DIM, (hh + 1) * ATT_HEAD_DIM)
                kh_ref[0, hh, b] = kb[:, cols].astype(BF16)
                vt_ref[0, hh, b, 0:ATT_HEAD_DIM, :] = vt[cols, rows].astype(BF16)
                vt_ref[0, hh, b, ATT_HEAD_DIM:VT_ROWS, :] = jnp.ones((VT_ROWS - ATT_HEAD_DIM, MOBA_BLOCK), BF16)


def _inproj(x, sh, sc, norm_w, w_pad, widths, tt, heads_out):
    n, t, d = x.shape
    per_token = sh.shape[1] != 1
    q_w, kv_w, conv_w_, z_w = widths
    offs = (0, q_w, q_w + kv_w, q_w + 2 * kv_w, q_w + 2 * kv_w + conv_w_, q_w + 2 * kv_w + conv_w_ + z_w)
    kv_heads = kv_w // ATT_HEAD_DIM
    nb_t = tt // MOBA_BLOCK
    mod_spec = (pl.BlockSpec((1, tt, d), lambda i, j: (i, j, 0)) if per_token
                else pl.BlockSpec((1, 1, d), lambda i, j: (i, 0, 0)))

    def tok(width):
        return pl.BlockSpec((1, tt, width), lambda i, j: (i, j, 0))

    out_specs = [tok(q_w), tok(kv_w), tok(kv_w), tok(conv_w_), tok(z_w), tok(LANES)]
    out_shape = [jax.ShapeDtypeStruct((n, t, w), F32) for w in (q_w, kv_w, kv_w, conv_w_, z_w, LANES)]
    if heads_out:
        nblk = t // MOBA_BLOCK
        out_specs += [
            pl.BlockSpec((1, kv_heads, nb_t, MOBA_BLOCK, ATT_HEAD_DIM), lambda i, j: (i, 0, j, 0, 0)),
            pl.BlockSpec((1, kv_heads, nb_t, VT_ROWS, MOBA_BLOCK), lambda i, j: (i, 0, j, 0, 0)),
            pl.BlockSpec((1, nb_t, 1, kv_w), lambda i, j: (i, j, 0, 0)),
        ]
        out_shape += [
            jax.ShapeDtypeStruct((n, kv_heads, nblk, MOBA_BLOCK, ATT_HEAD_DIM), BF16),
            jax.ShapeDtypeStruct((n, kv_heads, nblk, VT_ROWS, MOBA_BLOCK), BF16),
            jax.ShapeDtypeStruct((n, nblk, 1, kv_w), F32),
        ]
    return pl.pallas_call(
        functools.partial(_inproj_kernel, offs=offs, heads_out=heads_out, kv_heads=kv_heads),
        grid=(n, t // tt),
        in_specs=[pl.BlockSpec((1, tt, d), lambda i, j: (i, j, 0)), mod_spec, mod_spec,
                  pl.BlockSpec((1, d), lambda i, j: (0, 0)),
                  pl.BlockSpec(w_pad.shape, lambda i, j: (0, 0))],
        out_specs=out_specs,
        out_shape=out_shape,
        compiler_params=_cparams(("parallel", "parallel")),
        name="in_proj_prompt" if heads_out else "in_proj_sample",
    )(x, sh, sc, norm_w.reshape(1, d), w_pad)


def _bias_tile_kernel(rb_ref, o_ref):
    kvh = pl.program_id(0)
    delta = pl.num_programs(1) - 1 - pl.program_id(1)
    key = lax.broadcasted_iota(jnp.int32, (MOBA_BLOCK, MOBA_BLOCK), 0)
    qry = lax.broadcasted_iota(jnp.int32, (MOBA_BLOCK, MOBA_BLOCK), 1)
    bucket = _rel_bucket(delta * MOBA_BLOCK + qry - key)
    for g in range(ATT_GROUP):
        acc = jnp.zeros((MOBA_BLOCK, MOBA_BLOCK), F32)
        for t in range(REL_BUCKETS):
            acc = jnp.where(bucket == t, rb_ref[t, kvh * ATT_GROUP + g], acc)
        o_ref[0, 0, :, g * MOBA_BLOCK:(g + 1) * MOBA_BLOCK] = acc * LOG2E


def _bias_tiles(rel_bias, nblk):
    kv_heads = rel_bias.shape[1] // ATT_GROUP
    width = ATT_GROUP * MOBA_BLOCK
    return pl.pallas_call(
        _bias_tile_kernel,
        grid=(kv_heads, nblk),
        in_specs=[pl.BlockSpec(memory_space=pltpu.SMEM)],
        out_specs=pl.BlockSpec((1, 1, MOBA_BLOCK, width), lambda h, dlt: (h, dlt, 0, 0)),
        out_shape=jax.ShapeDtypeStruct((kv_heads, nblk, MOBA_BLOCK, width), F32),
        compiler_params=_cparams(("parallel", "parallel")),
        name="rel_bias_tiles",
    )(rel_bias)


def _bias_rows_kernel(tab_ref, o_ref, *, past, dec_seq):
    rows, width = o_ref.shape
    row = lax.broadcasted_iota(jnp.int32, (rows, width), 0)
    key = lax.broadcasted_iota(jnp.int32, (rows, width), 1)
    bucket = _rel_bucket(past + lax.rem(row, dec_seq) - key)
    acc = jnp.zeros((rows, width), F32)
    for t in range(REL_BUCKETS):
        acc = jnp.where(bucket == t, tab_ref[:, t:t + 1], acc)
    o_ref[...] = acc


def _bias_rows(rel_bias, past, dec_seq, width):
    heads = rel_bias.shape[1]
    rows = heads * dec_seq
    tab = jnp.repeat(rel_bias.T, dec_seq, axis=0)
    return pl.pallas_call(
        functools.partial(_bias_rows_kernel, past=past, dec_seq=dec_seq),
        out_shape=jax.ShapeDtypeStruct((rows, width), F32),
        name="rel_bias_rows",
    )(tab)


def _topk_rank_rows(gm, idx, n):
    rank = jnp.zeros(gm.shape, jnp.int32)
    for m in range(n):
        row = gm[m:m + 1, :]
        beats = (row > gm) | ((row == gm) & (m < idx))
        rank = rank + beats.astype(jnp.int32)
    return rank


def _attn_prompt_kernel(q_ref, kh_ref, vt_ref, km_ref, bias_ref, o_ref, sel_ref, raw_ref, *, nblk):
    qt = pl.program_id(2)
    q_t = q_ref[0].T
    tq = q_t.shape[1]
    width = ATT_GROUP * tq
    q_all = jnp.concatenate([q_t[g * ATT_HEAD_DIM:(g + 1) * ATT_HEAD_DIM, :] for g in range(ATT_GROUP)], axis=1)
    blk = lax.broadcasted_iota(jnp.int32, (nblk, width), 0)
    past = blk < qt
    gate = _dot(km_ref[0, 0], q_all, HIGHEST)
    rank = _topk_rank_rows(jnp.where(past, gate, NEG), blk, nblk)
    sel_ref[...] = (past & (rank < MOBA_TOPK)).astype(F32)
    qs = (q_all * (ATT_HEAD_DIM ** -0.5 * LOG2E)).astype(BF16)
    key = lax.broadcasted_iota(jnp.int32, (MOBA_BLOCK, width), 0)
    qry = lax.rem(lax.broadcasted_iota(jnp.int32, (MOBA_BLOCK, width), 1), tq)

    own_tile = nblk - 1
    s = jnp.where(key <= qry, _dot(kh_ref[0, 0, qt], qs) + bias_ref[0, own_tile], NEG)
    m0 = jnp.max(s, axis=0, keepdims=True)
    carry = (m0, _dot(vt_ref[0, 0, qt], jnp.exp2(s - m0).astype(BF16)))

    def update(carry, s, vt):
        m, acc = carry
        m_new = jnp.maximum(m, jnp.max(s, axis=0, keepdims=True))
        return m_new, jnp.exp2(m - m_new) * acc + _dot(vt, jnp.exp2(s - m_new).astype(BF16))

    def single(carry):
        s = _dot(kh_ref[0, 0, 0], qs) + bias_ref[0, own_tile - qt]
        return update(carry, jnp.where(sel_ref[0:1, :] > 0.5, s, NEG), vt_ref[0, 0, 0])

    odd = lax.rem(qt, 2)
    carry = lax.cond(odd == 1, single, lambda c: c, carry)

    n_pairs = qt // 2

    def raw_scores(i):
        kb = jnp.minimum(odd + 2 * jnp.minimum(i, n_pairs - 1), nblk - 2)
        kb = jnp.maximum(kb, 0)
        return _dot(kh_ref[0, 0, pl.ds(kb, 2)].reshape(2 * MOBA_BLOCK, ATT_HEAD_DIM), qs)

    raw_ref[...] = raw_scores(0)

    def pair(i, carry):
        kb = odd + 2 * i
        s = raw_ref[...] + bias_ref[0, pl.ds(own_tile - qt + kb, 2)].reshape(2 * MOBA_BLOCK, width)
        raw_ref[...] = raw_scores(i + 1)
        s = jnp.concatenate([jnp.where(sel_ref[pl.ds(kb, 1), :] > 0.5, s[:MOBA_BLOCK], NEG),
                             jnp.where(sel_ref[pl.ds(kb + 1, 1), :] > 0.5, s[MOBA_BLOCK:], NEG)], axis=0)
        vt2 = jnp.concatenate([vt_ref[0, 0, kb], vt_ref[0, 0, kb + 1]], axis=1)
        return update(carry, s, vt2)

    _, acc = lax.fori_loop(0, n_pairs, pair, carry)
    out_t = acc[:ATT_HEAD_DIM] / acc[ATT_HEAD_DIM:ATT_HEAD_DIM + 1]
    o_ref[0] = jnp.concatenate([out_t[:, g * tq:(g + 1) * tq].T for g in range(ATT_GROUP)], axis=1)


def _attn_prompt(q, kh, vt, km, bias):
    n, t, q_w = q.shape
    kv_heads, nblk = kh.shape[1], kh.shape[2]
    assert nblk >= 2, "the look-ahead in the block-pair loop reads two key blocks"
    gw = ATT_GROUP * ATT_HEAD_DIM
    return pl.pallas_call(
        functools.partial(_attn_prompt_kernel, nblk=nblk),
        grid=(n, kv_heads, nblk),
        in_specs=[
            pl.BlockSpec((1, MOBA_BLOCK, gw), lambda i, h, j: (i, j, h)),
            pl.BlockSpec((1, 1, nblk, MOBA_BLOCK, ATT_HEAD_DIM), lambda i, h, j: (i, h, 0, 0, 0)),
            pl.BlockSpec((1, 1, nblk, VT_ROWS, MOBA_BLOCK), lambda i, h, j: (i, h, 0, 0, 0)),
            pl.BlockSpec((1, 1, nblk, ATT_HEAD_DIM), lambda i, h, j: (i, h, 0, 0)),
            pl.BlockSpec((1, nblk, MOBA_BLOCK, ATT_GROUP * MOBA_BLOCK), lambda i, h, j: (h, 0, 0, 0)),
        ],
        out_specs=pl.BlockSpec((1, MOBA_BLOCK, gw), lambda i, h, j: (i, j, h)),
        out_shape=jax.ShapeDtypeStruct((n, t, q_w), F32),
        scratch_shapes=[pltpu.VMEM((nblk, ATT_GROUP * MOBA_BLOCK), F32),
                        pltpu.VMEM((2 * MOBA_BLOCK, ATT_GROUP * MOBA_BLOCK), F32)],
        compiler_params=_cparams(("parallel", "parallel", "arbitrary")),
        name="moba_prompt",
    )(q, kh, vt, km, bias)


def _attn_sample_kernel(pt_ref, qc_ref, knew_ref, vnew_ref, bias_ref, ck_hbm, cv_hbm, o_ref,
                        kbuf, vbuf, sem, *, n_pages, dec_seq, kv_heads):
    s = pl.program_id(0)
    nseq = pl.num_programs(0)
    slot = lax.rem(s, 2)
    page = kbuf.shape[3]
    pages_per_blk = MOBA_BLOCK // page
    nblk = n_pages // pages_per_blk
    past_len = nblk * MOBA_BLOCK

    def start_fetch(seq, sl):
        def one(p, carry):
            pg = pt_ref[seq, p]
            pltpu.make_async_copy(ck_hbm.at[pg], kbuf.at[sl, p], sem.at[0, sl]).start()
            pltpu.make_async_copy(cv_hbm.at[pg], vbuf.at[sl, p], sem.at[1, sl]).start()
            return carry
        lax.fori_loop(0, n_pages, one, 0)

    @pl.when(s == 0)
    def _():
        start_fetch(0, 0)

    @pl.when(s + 1 < nseq)
    def _():
        start_fetch(s + 1, 1 - slot)

    pltpu.make_async_copy(ck_hbm.at[pl.ds(0, n_pages)], kbuf.at[slot], sem.at[0, slot]).wait()
    pltpu.make_async_copy(cv_hbm.at[pl.ds(0, n_pages)], vbuf.at[slot], sem.at[1, slot]).wait()

    qc = qc_ref[0]
    rows, kv_w = qc.shape
    scale = ATT_HEAD_DIM ** -0.5
    q_hi, q_lo = _split(qc)
    q_both = jnp.concatenate([q_hi, q_lo], axis=0)

    s_pages = []
    for p in range(n_pages):
        k_hi, k_lo = _split(kbuf[slot, p])
        t = _dot(q_both, k_hi)
        s_pages.append(t[:rows] + t[rows:] + _dot(q_hi, k_lo))

    lane = lax.broadcasted_iota(jnp.int32, (rows, LANES), 1)
    gate = jnp.zeros((rows, LANES), F32)
    for b in range(nblk):
        tot = s_pages[b * pages_per_blk]
        for p in range(b * pages_per_blk + 1, (b + 1) * pages_per_blk):
            tot = tot + s_pages[p]
        gate = jnp.where(lane == b, jnp.sum(tot, axis=1, keepdims=True) * (1.0 / MOBA_BLOCK), gate)
    rank = jnp.zeros((rows, LANES), jnp.int32)
    for m in range(nblk):
        col = gate[:, m:m + 1]
        rank = rank + ((col > gate) | ((col == gate) & (m < lane))).astype(jnp.int32)
    sel = rank < MOBA_TOPK

    masked = []
    for p in range(n_pages):
        b = p // pages_per_blk
        sp = s_pages[p] * scale + bias_ref[:, p * page:(p + 1) * page]
        masked.append(jnp.where(sel[:, b:b + 1], sp, NEG))
    n_new = knew_ref.shape[1]
    r_idx = lax.rem(lax.broadcasted_iota(jnp.int32, (rows, n_new), 0), dec_seq)
    j_idx = lax.broadcasted_iota(jnp.int32, (rows, n_new), 1)
    s_own = _dot_nt(q_hi, knew_ref[0].astype(BF16)) * scale + bias_ref[:, past_len:past_len + n_new]
    s_own = jnp.where(j_idx <= r_idx, s_own, NEG)
    m = jnp.max(s_own, axis=1, keepdims=True)
    for sp in masked:
        m = jnp.maximum(m, jnp.max(sp, axis=1, keepdims=True))
    p_own = jnp.exp(s_own - m)
    l = jnp.sum(p_own, axis=1, keepdims=True)
    acc = _dot(p_own.astype(BF16), vnew_ref[0].astype(BF16))
    for p, sp in enumerate(masked):
        prob = jnp.exp(sp - m)
        l = l + jnp.sum(prob, axis=1, keepdims=True)
        acc = acc + _dot_nt(prob.astype(BF16), vbuf[slot, p].astype(BF16))
    out = acc / l
    rpk = rows // kv_heads
    for h in range(kv_heads):
        o_ref[0, h * rpk:(h + 1) * rpk, :] = out[h * rpk:(h + 1) * rpk,
                                                 h * ATT_HEAD_DIM:(h + 1) * ATT_HEAD_DIM]


def _attn_sample(page_table, qc, k_new, v_new, bias, cache_k, cache_v, dec_seq, kv_heads):
    b, rows, kv_w = qc.shape
    n_pages = page_table.shape[1]
    page = cache_k.shape[2]
    n_new = k_new.shape[1]
    grid_spec = pltpu.PrefetchScalarGridSpec(
        num_scalar_prefetch=1,
        grid=(b,),
        in_specs=[
            pl.BlockSpec((1, rows, kv_w), lambda i, pt: (i, 0, 0)),
            pl.BlockSpec((1, n_new, kv_w), lambda i, pt: (i, 0, 0)),
            pl.BlockSpec((1, n_new, kv_w), lambda i, pt: (i, 0, 0)),
            pl.BlockSpec(bias.shape, lambda i, pt: (0, 0)),
            pl.BlockSpec(memory_space=pl.ANY),
            pl.BlockSpec(memory_space=pl.ANY),
        ],
        out_specs=pl.BlockSpec((1, rows, ATT_HEAD_DIM), lambda i, pt: (i, 0, 0)),
        scratch_shapes=[pltpu.VMEM((2, n_pages, kv_w, page), F32),
                        pltpu.VMEM((2, n_pages, kv_w, page), F32),
                        pltpu.SemaphoreType.DMA((2, 2))],
    )
    return pl.pallas_call(
        functools.partial(_attn_sample_kernel, n_pages=n_pages, dec_seq=dec_seq, kv_heads=kv_heads),
        grid_spec=grid_spec,
        out_shape=jax.ShapeDtypeStruct((b, rows, ATT_HEAD_DIM), F32),
        compiler_params=_cparams(("arbitrary",)),
        name="moba_sample",
    )(page_table, qc, k_new, v_new, bias, cache_k, cache_v)


def _gdn_gates(ba, alog, dt):
    return jax.nn.sigmoid(ba), -jnp.exp(alog) * _softplus(ba + dt)


def _gdn_prompt_kernel(x_ref, ba_ref, bat_ref, cw_ref, alog_ref, dt_ref, alogt_ref, dtt_ref,
                       o_ref, s_ref, ext_ref, *, heads):
    ct, cw = x_ref.shape[1], x_ref.shape[2]
    qk_w = heads * GDN_DK
    cs = GDN_CHUNK
    halo = SUBLANES

    @pl.when(pl.program_id(1) == 0)
    def _():
        ext_ref[0:halo, :] = jnp.zeros((halo, cw), F32)
        s_ref[...] = jnp.zeros(s_ref.shape, F32)

    x = x_ref[0]
    ext_ref[halo:halo + ct, :] = x
    w = cw_ref[...]
    conv = x * w[GDN_CONV - 1:GDN_CONV, :]
    for i in range(GDN_CONV - 1):
        conv = conv + ext_ref[pl.ds(halo - (GDN_CONV - 1) + i, ct), :] * w[i:i + 1, :]
    ext_ref[0:halo, :] = x[ct - halo:ct, :]
    act = _silu(conv)

    beta_f, g_f = _gdn_gates(ba_ref[0], alog_ref[...], dt_ref[...])
    _, g_t = _gdn_gates(bat_ref[0], alogt_ref[...], dtt_ref[...])

    r_i = lax.broadcasted_iota(jnp.int32, (cs, cs), 0)
    c_i = lax.broadcasted_iota(jnp.int32, (cs, cs), 1)
    tril = r_i >= c_i
    strict = r_i > c_i
    eye = (r_i == c_i).astype(F32)
    cum_l = tril.astype(F32)
    cum_u = (r_i <= c_i).astype(F32)

    qn, kn, vv = [], [], []
    for h in range(heads):
        qh = act[:, h * GDN_DK:(h + 1) * GDN_DK]
        kh = act[:, qk_w + h * GDN_DK:qk_w + (h + 1) * GDN_DK]
        qn.append(qh * lax.rsqrt(jnp.sum(qh * qh, axis=-1, keepdims=True) + EPS) * (GDN_DK ** -0.5))
        kn.append(kh * lax.rsqrt(jnp.sum(kh * kh, axis=-1, keepdims=True) + EPS))
        vv.append(act[:, 2 * qk_w + h * GDN_DK:2 * qk_w + (h + 1) * GDN_DK])

    n_chunks = ct // cs
    units = [(c, h) for c in range(n_chunks) for h in range(heads)]
    gc, gct = [], []
    for c in range(n_chunks):
        rows = slice(c * cs, (c + 1) * cs)
        gc.append(_dot(cum_l, g_f[rows, :], HIGHEST))
        gct.append(_dot(g_t[:, rows], cum_u, HIGHEST))
    q_u, k_u, kbeta_u, vbeta_u, gcc_u, egc_u, decay_u = {}, {}, {}, {}, {}, {}, {}
    for (c, h) in units:
        rows = slice(c * cs, (c + 1) * cs)
        beta = beta_f[rows, h:h + 1]
        gcc = gc[c][:, heads + h:heads + h + 1]
        gcr = gct[c][heads + h:heads + h + 1, :]
        q_u[c, h], k_u[c, h] = qn[h][rows], kn[h][rows]
        kbeta_u[c, h] = k_u[c, h] * beta
        vbeta_u[c, h] = vv[h][rows] * beta
        gcc_u[c, h], egc_u[c, h] = gcc, jnp.exp(gcc)
        decay_u[c, h] = jnp.where(tril, jnp.exp(jnp.where(tril, gcc - gcr, 0.0)), 0.0)
    a_u = {u: _mm3_nt(jnp.concatenate([kbeta_u[u], q_u[u]], axis=0), k_u[u]) for u in units}
    x_u = {u: -jnp.where(strict, a_u[u][:cs] * decay_u[u], 0.0) for u in units}
    intra_u = {u: jnp.where(tril, a_u[u][cs:] * decay_u[u], 0.0) for u in units}
    p_u = {u: eye + x_u[u] for u in units}
    xp_u = {u: _mm3(x_u[u], x_u[u]) for u in units}
    npow = 2
    while npow * 2 < cs:
        pr_u = {u: _mm3(jnp.concatenate([p_u[u], xp_u[u]], axis=0), xp_u[u]) for u in units}
        p_u = {u: p_u[u] + pr_u[u][:cs] for u in units}
        xp_u = {u: pr_u[u][cs:] for u in units}
        npow *= 2
    p_u = {u: p_u[u] + _mm3(p_u[u], xp_u[u]) for u in units}
    uw_u = {u: _dot(p_u[u].astype(BF16),
                    jnp.concatenate([vbeta_u[u], kbeta_u[u] * egc_u[u]], axis=1).astype(BF16)) for u in units}

    state = [s_ref[0, h] for h in range(heads)]
    for c in range(n_chunks):
        rows = slice(c * cs, (c + 1) * cs)
        ws = [_dot(jnp.concatenate([uw_u[c, h][:, GDN_DK:], q_u[c, h] * egc_u[c, h]], axis=0).astype(BF16),
                   state[h].astype(BF16)) for h in range(heads)]
        v_new = [uw_u[c, h][:, :GDN_DK] - ws[h][:cs] for h in range(heads)]
        v_bf = [v.astype(BF16) for v in v_new]
        for h in range(heads):
            o_ref[0, rows, h * GDN_DK:(h + 1) * GDN_DK] = ws[h][cs:] + _dot(intra_u[c, h].astype(BF16), v_bf[h])
        for h in range(heads):
            gcc = gcc_u[c, h]
            g_last = gcc[cs - 1:cs, :]
            state[h] = state[h] * jnp.exp(g_last) + _dot_tn(
                (k_u[c, h] * jnp.exp(g_last - gcc)).astype(BF16), v_bf[h])
    for h in range(heads):
        s_ref[0, h] = state[h]


def _gdn_prompt(conv_pre, ba, conv_w, a_log, dt_bias, heads, ct):
    n, t, cw = conv_pre.shape
    bat = jnp.swapaxes(ba[:, :, :2 * heads], 1, 2)
    pad = jnp.zeros((heads,), F32)
    alog = jnp.concatenate([pad, a_log])
    dt = jnp.concatenate([pad, dt_bias])
    lane_pad = (0, LANES - 2 * heads)
    return pl.pallas_call(
        functools.partial(_gdn_prompt_kernel, heads=heads),
        grid=(n, t // ct),
        in_specs=[
            pl.BlockSpec((1, ct, cw), lambda i, j: (i, j, 0)),
            pl.BlockSpec((1, ct, LANES), lambda i, j: (i, j, 0)),
            pl.BlockSpec((1, 2 * heads, ct), lambda i, j: (i, 0, j)),
            pl.BlockSpec(conv_w.shape, lambda i, j: (0, 0)),
            pl.BlockSpec((1, LANES), lambda i, j: (0, 0)),
            pl.BlockSpec((1, LANES), lambda i, j: (0, 0)),
            pl.BlockSpec((2 * heads, 1), lambda i, j: (0, 0)),
            pl.BlockSpec((2 * heads, 1), lambda i, j: (0, 0)),
        ],
        out_specs=[pl.BlockSpec((1, ct, heads * GDN_DK), lambda i, j: (i, j, 0)),
                   pl.BlockSpec((1, heads, GDN_DK, GDN_DK), lambda i, j: (i, 0, 0, 0))],
        out_shape=[jax.ShapeDtypeStruct((n, t, heads * GDN_DK), F32),
                   jax.ShapeDtypeStruct((n, heads, GDN_DK, GDN_DK), F32)],
        scratch_shapes=[pltpu.VMEM((SUBLANES + ct, cw), F32)],
        compiler_params=_cparams(("parallel", "arbitrary")),
        name="gdn_prompt",
    )(conv_pre, ba, bat, conv_w, jnp.pad(alog, lane_pad).reshape(1, LANES),
      jnp.pad(dt, lane_pad).reshape(1, LANES), alog.reshape(2 * heads, 1), dt.reshape(2 * heads, 1))


def _gdn_sample_kernel(ext_ref, ba_ref, cw_ref, alog_ref, dt_ref, s_in_ref, o_ref, s_out_ref, *, heads):
    t_len = ba_ref.shape[0]
    gs = ext_ref.shape[1]
    qk_w = heads * GDN_DK
    w = cw_ref[...]
    q_t, k_t, v_r, beta_r, dec_r = [], [], [], [], []
    for t in range(t_len):
        conv = ext_ref[t] * w[0:1, :]
        for i in range(1, GDN_CONV):
            conv = conv + ext_ref[t + i] * w[i:i + 1, :]
        act = _silu(conv)
        beta_f, g_f = _gdn_gates(ba_ref[t], alog_ref[...], dt_ref[...])
        beta_r.append(beta_f)
        dec_r.append(jnp.exp(g_f))
        qs, ks, vs = [], [], []
        for h in range(heads):
            qh = act[:, h * GDN_DK:(h + 1) * GDN_DK]
            kh = act[:, qk_w + h * GDN_DK:qk_w + (h + 1) * GDN_DK]
            qn = qh * lax.rsqrt(jnp.sum(qh * qh, axis=-1, keepdims=True) + EPS) * (GDN_DK ** -0.5)
            kn = kh * lax.rsqrt(jnp.sum(kh * kh, axis=-1, keepdims=True) + EPS)
            qs.append(qn.T)
            ks.append(kn.T)
            vs.append(act[:, 2 * qk_w + h * GDN_DK:2 * qk_w + (h + 1) * GDN_DK])
        q_t.append(qs)
        k_t.append(ks)
        v_r.append(vs)
    for i in range(gs):
        for h in range(heads):
            st = s_in_ref[i, h]
            for t in range(t_len):
                st = st * dec_r[t][i:i + 1, heads + h:heads + h + 1]
                kcol = k_t[t][h][:, i:i + 1]
                ks_row = jnp.sum(st * kcol, axis=0, keepdims=True)
                delta = (v_r[t][h][i:i + 1, :] - ks_row) * beta_r[t][i:i + 1, h:h + 1]
                st = st + kcol * delta
                o_ref[t, i:i + 1, h * GDN_DK:(h + 1) * GDN_DK] = jnp.sum(
                    st * q_t[t][h][:, i:i + 1], axis=0, keepdims=True)
            s_out_ref[i, h] = st


def _gdn_sample(ext_tm, ba_tm, conv_w, a_log, dt_bias, state, heads, gs):
    t_len, b, _ = ba_tm.shape
    cw = ext_tm.shape[2]
    pad = jnp.zeros((heads,), F32)
    lane_pad = (0, LANES - 2 * heads)
    alog = jnp.pad(jnp.concatenate([pad, a_log]), lane_pad).reshape(1, LANES)
    dt = jnp.pad(jnp.concatenate([pad, dt_bias]), lane_pad).reshape(1, LANES)
    return pl.pallas_call(
        functools.partial(_gdn_sample_kernel, heads=heads),
        grid=(b // gs,),
        in_specs=[
            pl.BlockSpec((ext_tm.shape[0], gs, cw), lambda i: (0, i, 0)),
            pl.BlockSpec((t_len, gs, LANES), lambda i: (0, i, 0)),
            pl.BlockSpec(conv_w.shape, lambda i: (0, 0)),
            pl.BlockSpec((1, LANES), lambda i: (0, 0)),
            pl.BlockSpec((1, LANES), lambda i: (0, 0)),
            pl.BlockSpec((gs, heads, GDN_DK, GDN_DK), lambda i: (i, 0, 0, 0)),
        ],
        out_specs=[pl.BlockSpec((t_len, gs, heads * GDN_DK), lambda i: (0, i, 0)),
                   pl.BlockSpec((gs, heads, GDN_DK, GDN_DK), lambda i: (i, 0, 0, 0))],
        out_shape=[jax.ShapeDtypeStruct((t_len, b, heads * GDN_DK), F32),
                   jax.ShapeDtypeStruct(state.shape, F32)],
        compiler_params=_cparams(("parallel",)),
        name="gdn_sample",
    )(ext_tm, ba_tm, conv_w, alog, dt, state)


def _mid_kernel(x_ref, attn_ref, og_ref, z_ref, gta_ref, scf_ref, shf_ref, gw_ref, wout_ref, nf_ref,
                wr_ref, br_ref, x1_ref, h2_ref, gates_ref, topi_ref=None, *, heads, grouped):
    og = og_ref[0]
    z = z_ref[0]
    parts = [attn_ref[0]]
    for h in range(heads):
        o = og[:, h * GDN_DK:(h + 1) * GDN_DK]
        y = o * lax.rsqrt(jnp.mean(o * o, axis=-1, keepdims=True) + EPS) * gw_ref[...]
        parts.append(y * _silu(z[:, h * GDN_DK:(h + 1) * GDN_DK]))
    mix = jnp.concatenate(parts, axis=1).astype(BF16)
    x1 = x_ref[0] + gta_ref[0] * _dot(mix, wout_ref[...])
    x1_ref[0] = x1
    h2 = (x1 * lax.rsqrt(jnp.mean(x1 * x1, axis=-1, keepdims=True) + EPS) * nf_ref[...]
          * (1.0 + scf_ref[0]) + shf_ref[0])
    logits = _dot(h2, wr_ref[...], HIGHEST) + br_ref[...]
    lane = lax.broadcasted_iota(jnp.int32, logits.shape, 1)
    work = logits
    sel = jnp.zeros(logits.shape, jnp.bool_)
    vals, idxs = [], []
    for j in range(TOP_K):
        mx = jnp.max(work, axis=-1, keepdims=True)
        idx = jnp.min(jnp.where(work == mx, lane, LANES), axis=-1, keepdims=True)
        pick = lane == idx
        vals.append(mx)
        idxs.append(idx)
        sel = sel | pick
        work = jnp.where(pick, -jnp.inf, work)
    if grouped:
        tt = h2.shape[0]
        for s in range(h2.shape[1] // LANES):
            h2_ref[0, pl.ds(s, tt, stride=h2.shape[1] // LANES), :] = h2[:, s * LANES:(s + 1) * LANES]
        ex = [jnp.exp(v - vals[0]) for v in vals]
        denom = ex[0]
        for v in ex[1:]:
            denom = denom + v
        topi = jnp.zeros(logits.shape, jnp.int32)
        topw = jnp.zeros(logits.shape, F32)
        for j in range(TOP_K):
            topi = jnp.where(lane == j, idxs[j], topi)
            topw = jnp.where(lane == j, ex[j] / denom, topw)
        gates_ref[0] = topw
        topi_ref[0] = topi
    else:
        h2_ref[0] = h2.astype(BF16)
        e = jnp.where(sel, jnp.exp(logits - vals[0]), 0.0)
        gates_ref[0] = e / jnp.sum(e, axis=-1, keepdims=True)


def _mid(x, attn, og, z, gta, scf, shf, gdn_norm_w, w_out_bf, norm_ffn_w, wr_pad, br_pad, tt, heads, grouped):
    n, t, d = x.shape
    slab = d // LANES
    per_token = gta.shape[1] != 1
    mod_spec = (pl.BlockSpec((1, tt, d), lambda i, j: (i, j, 0)) if per_token
                else pl.BlockSpec((1, 1, d), lambda i, j: (i, 0, 0)))

    def tok(width):
        return pl.BlockSpec((1, tt, width), lambda i, j: (i, j, 0))

    def full(a):
        return pl.BlockSpec(a.shape, lambda i, j: (0,) * a.ndim)

    gw = gdn_norm_w.reshape(1, -1)
    nf = norm_ffn_w.reshape(1, d)
    if grouped:
        out_specs = [tok(d), pl.BlockSpec((1, tt * slab, LANES), lambda i, j: (i, j, 0)), tok(LANES), tok(LANES)]
        out_shape = [jax.ShapeDtypeStruct((n, t, d), F32), jax.ShapeDtypeStruct((n, t * slab, LANES), F32),
                     jax.ShapeDtypeStruct((n, t, LANES), F32), jax.ShapeDtypeStruct((n, t, LANES), jnp.int32)]
    else:
        out_specs = [tok(d), tok(d), tok(LANES)]
        out_shape = [jax.ShapeDtypeStruct((n, t, d), F32), jax.ShapeDtypeStruct((n, t, d), BF16),
                     jax.ShapeDtypeStruct((n, t, LANES), F32)]
    return pl.pallas_call(
        functools.partial(_mid_kernel, heads=heads, grouped=grouped),
        grid=(n, t // tt),
        in_specs=[tok(d), tok(attn.shape[2]), tok(og.shape[2]), tok(z.shape[2]), mod_spec, mod_spec, mod_spec,
                  full(gw), full(w_out_bf), full(nf), full(wr_pad), full(br_pad)],
        out_specs=out_specs,
        out_shape=out_shape,
        compiler_params=_cparams(("parallel", "parallel")),
        name="out_proj_router_grouped" if grouped else "out_proj_router",
    )(x, attn, og, z, gta, scf, shf, gw, w_out_bf, nf, wr_pad, br_pad)


def _expert_ffn(h, wup_ref, bup_ref, wdn_ref, bdn_ref, ff_chunk):
    d_ff = wdn_ref.shape[1]
    y = jnp.zeros((h.shape[0], wdn_ref.shape[2]), F32)
    for c in range(d_ff // ff_chunk):
        lo, hi = c * ff_chunk, (c + 1) * ff_chunk
        x_glu = _dot(h, wup_ref[0, :, lo:hi]) + bup_ref[0, :, lo:hi]
        x_lin = _dot(h, wup_ref[0, :, d_ff + lo:d_ff + hi]) + bup_ref[0, :, d_ff + lo:d_ff + hi]
        x_glu = jnp.minimum(x_glu, SWIGLU_LIMIT)
        x_lin = jnp.clip(x_lin, -SWIGLU_LIMIT, SWIGLU_LIMIT)
        act = x_glu * jax.nn.sigmoid(SWIGLU_ALPHA * x_glu) * (x_lin + 1.0)
        y = y + _dot(act.astype(BF16), wdn_ref[0, lo:hi, :])
    return y + bdn_ref[0]


def _moe_kernel(h2_ref, gates_ref, x1_ref, gtf_ref, wup_ref, bup_ref, wdn_ref, bdn_ref, nw_ref,
                y_ref, acc_ref, *, ff_chunk):
    e = pl.program_id(2)
    n_exp = pl.num_programs(2)

    @pl.when(e == 0)
    def _():
        acc_ref[...] = jnp.zeros(acc_ref.shape, F32)

    gates = gates_ref[0]
    lane = lax.broadcasted_iota(jnp.int32, gates.shape, 1)
    gcol = jnp.sum(jnp.where(lane == e, gates, 0.0), axis=-1, keepdims=True)
    acc_ref[...] += gcol * _expert_ffn(h2_ref[0], wup_ref, bup_ref, wdn_ref, bdn_ref, ff_chunk)

    @pl.when(e == n_exp - 1)
    def _():
        x = x1_ref[0] + gtf_ref[0] * acc_ref[...]
        y_ref[0] = x * lax.rsqrt(jnp.mean(x * x, axis=-1, keepdims=True) + EPS) * nw_ref[...]


def _moe(h2, gates, x1, gtf, w_up_bf, b_up, w_dn_bf, b_dn, norm_final_w, tm):
    n, t, d = x1.shape
    n_exp, _, ff2 = w_up_bf.shape
    d_ff = ff2 // 2
    per_token = gtf.shape[1] != 1
    mod_spec = (pl.BlockSpec((1, tm, d), lambda i, j, e: (i, j, 0)) if per_token
                else pl.BlockSpec((1, 1, d), lambda i, j, e: (i, 0, 0)))

    def tok(width):
        return pl.BlockSpec((1, tm, width), lambda i, j, e: (i, j, 0))

    return pl.pallas_call(
        functools.partial(_moe_kernel, ff_chunk=min(512, d_ff)),
        grid=(n, t // tm, n_exp),
        in_specs=[tok(d), tok(LANES), tok(d), mod_spec,
                  pl.BlockSpec((1, d, ff2), lambda i, j, e: (e, 0, 0)),
                  pl.BlockSpec((1, 1, ff2), lambda i, j, e: (e, 0, 0)),
                  pl.BlockSpec((1, d_ff, d), lambda i, j, e: (e, 0, 0)),
                  pl.BlockSpec((1, 1, d), lambda i, j, e: (e, 0, 0)),
                  pl.BlockSpec((1, d), lambda i, j, e: (0, 0))],
        out_specs=tok(d),
        out_shape=jax.ShapeDtypeStruct((n, t, d), F32),
        scratch_shapes=[pltpu.VMEM((tm, d), F32)],
        compiler_params=_cparams(("parallel", "parallel", "arbitrary")),
        name="moe_ffn",
    )(h2, gates, x1, gtf, w_up_bf, b_up.reshape(n_exp, 1, ff2), w_dn_bf, b_dn.reshape(n_exp, 1, d),
      norm_final_w.reshape(1, d))


MOE_GROUP = 4096
MOE_CHUNK = 128
MOE_SCATTER_BATCH = 8


def _moe_grouped_kernel(cnt_ref, off_ref, tok_ref, wgt_ref, xg_ref, wup_ref, bup_ref, wdn_ref, bdn_ref,
                        y_ref, xbuf, obuf, *, group, chunk, ff_chunk):
    g = pl.program_id(0)
    e = pl.program_id(1)
    slab = xbuf.shape[0] // chunk

    @pl.when(e == 0)
    def _():
        y_ref[...] = jnp.zeros(y_ref.shape, F32)
        obuf[...] = jnp.zeros(obuf.shape, F32)

    seg = g * pl.num_programs(1) + e
    off = off_ref[seg]
    end = off + cnt_ref[seg]
    n_chunks = (end - off + chunk - 1) // chunk

    def gather(r0):
        for i in range(chunk):
            tok = tok_ref[0, 0, r0 + i]
            xbuf[pl.ds(i * slab, slab), :] = xg_ref[0, pl.ds(pl.multiple_of(tok * slab, slab), slab), :]

    def scatter(r0, live):
        lim = jnp.where(live, end, 0)
        for b0 in range(0, chunk, MOE_SCATTER_BATCH):
            rows, wgts = [], []
            for i in range(b0, b0 + MOE_SCATTER_BATCH):
                valid = r0 + i < lim
                tok = jnp.where(valid, tok_ref[0, 0, r0 + i], group)
                rows.append(pl.multiple_of(tok * slab, slab))
                wgts.append(jnp.where(valid, wgt_ref[0, 0, r0 + i], 0.0))
            cur = [y_ref[0, pl.ds(r, slab), :] for r in rows]
            new = [cur[k] + wgts[k] * obuf[pl.ds((b0 + k) * slab, slab), :] for k in range(MOE_SCATTER_BATCH)]
            for k in range(MOE_SCATTER_BATCH):
                y_ref[0, pl.ds(rows[k], slab), :] = new[k]

    @pl.when(n_chunks > 0)
    def _():
        gather(off)

    def one_pass(j, carry):
        r0 = off + j * chunk
        x = jnp.concatenate([xbuf[pl.ds(s, chunk, stride=slab), :] for s in range(slab)], axis=1)
        gather(r0 + chunk)
        scatter(r0 - chunk, j > 0)
        y = _expert_ffn(x.astype(BF16), wup_ref, bup_ref, wdn_ref, bdn_ref, ff_chunk)
        for s in range(slab):
            obuf[pl.ds(s, chunk, stride=slab), :] = y[:, s * LANES:(s + 1) * LANES]
        return carry

    lax.fori_loop(0, n_chunks, one_pass, 0)

    @pl.when(n_chunks > 0)
    def _():
        scatter(off + (n_chunks - 1) * chunk, True)


def _moe_grouped(h2_slabs, topw, topi, w_up_bf, b_up, w_dn_bf, b_dn, group):
    m = topw.shape[0]
    slab = h2_slabs.shape[0] // m
    n_exp, d, ff2 = w_up_bf.shape
    d_ff = ff2 // 2
    ngroups = m // group
    rows = group * TOP_K
    eid = topi[:, :TOP_K].reshape(ngroups, rows)
    order = jnp.argsort(eid, axis=1, stable=True)
    tok_sorted = (order // TOP_K).astype(jnp.int32)
    wgt_sorted = jnp.take_along_axis(topw[:, :TOP_K].reshape(ngroups, rows), order, axis=1)
    counts = jnp.sum((eid[:, :, None] == jnp.arange(n_exp, dtype=jnp.int32)).astype(jnp.int32), axis=1)
    offs = jnp.cumsum(counts, axis=1) - counts + MOE_CHUNK
    table = rows + 3 * MOE_CHUNK
    pad = ((0, 0), (MOE_CHUNK, 2 * MOE_CHUNK))
    tok_sorted = jnp.pad(tok_sorted, pad).reshape(ngroups, 1, table)
    wgt_sorted = jnp.pad(wgt_sorted, pad).reshape(ngroups, 1, table)

    grid_spec = pltpu.PrefetchScalarGridSpec(
        num_scalar_prefetch=2,
        grid=(ngroups, n_exp),
        in_specs=[
            pl.BlockSpec((1, 1, table), lambda g, e, c, o: (g, 0, 0), memory_space=pltpu.SMEM),
            pl.BlockSpec((1, 1, table), lambda g, e, c, o: (g, 0, 0), memory_space=pltpu.SMEM),
            pl.BlockSpec((1, group * slab, LANES), lambda g, e, c, o: (g, 0, 0), pipeline_mode=pl.Buffered(1)),
            pl.BlockSpec((1, d, ff2), lambda g, e, c, o: (e, 0, 0)),
            pl.BlockSpec((1, 1, ff2), lambda g, e, c, o: (e, 0, 0)),
            pl.BlockSpec((1, d_ff, d), lambda g, e, c, o: (e, 0, 0)),
            pl.BlockSpec((1, 1, d), lambda g, e, c, o: (e, 0, 0)),
        ],
        out_specs=pl.BlockSpec((1, (group + 1) * slab, LANES), lambda g, e, c, o: (g, 0, 0),
                               pipeline_mode=pl.Buffered(1)),
        scratch_shapes=[pltpu.VMEM((MOE_CHUNK * slab, LANES), F32), pltpu.VMEM((MOE_CHUNK * slab, LANES), F32)],
    )
    return pl.pallas_call(
        functools.partial(_moe_grouped_kernel, group=group, chunk=MOE_CHUNK, ff_chunk=min(512, d_ff)),
        grid_spec=grid_spec,
        out_shape=jax.ShapeDtypeStruct((ngroups, (group + 1) * slab, LANES), F32),
        compiler_params=pltpu.CompilerParams(dimension_semantics=("parallel", "arbitrary"),
                                             vmem_limit_bytes=MOE_VMEM_LIMIT),
        name="moe_grouped",
    )(counts.reshape(-1), offs.reshape(-1), tok_sorted, wgt_sorted,
      h2_slabs.reshape(ngroups, group * slab, LANES), w_up_bf, b_up.reshape(n_exp, 1, ff2), w_dn_bf,
      b_dn.reshape(n_exp, 1, d))


def _moe_finish_kernel(moe_ref, x1_ref, gtf_ref, nw_ref, y_ref):
    tt, d = x1_ref.shape[1], x1_ref.shape[2]
    slab = d // LANES
    moe = jnp.concatenate([moe_ref[0, pl.ds(s, tt, stride=slab), :] for s in range(slab)], axis=1)
    x = x1_ref[0] + gtf_ref[0] * moe
    y_ref[0] = x * lax.rsqrt(jnp.mean(x * x, axis=-1, keepdims=True) + EPS) * nw_ref[...]


def _moe_finish(moe_slabs, x1, gtf, norm_final_w, group, tt):
    n, t, d = x1.shape
    slab = d // LANES
    per_group = group // tt
    steps = t // tt
    return pl.pallas_call(
        _moe_finish_kernel,
        grid=(n, steps),
        in_specs=[
            pl.BlockSpec((1, tt * slab, LANES),
                         lambda i, j: ((i * steps + j) // per_group, (i * steps + j) % per_group, 0)),
            pl.BlockSpec((1, tt, d), lambda i, j: (i, j, 0)),
            pl.BlockSpec((1, 1, d), lambda i, j: (i, 0, 0)),
            pl.BlockSpec((1, d), lambda i, j: (0, 0)),
        ],
        out_specs=pl.BlockSpec((1, tt, d), lambda i, j: (i, j, 0)),
        out_shape=jax.ShapeDtypeStruct((n, t, d), F32),
        compiler_params=_cparams(("parallel", "parallel")),
        name="moe_finish",
    )(moe_slabs, x1, gtf, norm_final_w.reshape(1, d))


def _pick_tile(t, pref):
    tile = min(t, pref)
    assert t % tile == 0, (t, tile)
    return tile


def kernel(x_prompt, x_sample, cache_k, cache_v, state_conv, state_ssm, page_table, c_prompt, c_sample, w_ada, b_ada, norm_attn_w, norm_ffn_w, norm_final_w, w_in, rel_bias, conv_w, a_log, dt_bias, gdn_norm_w, w_out, w_router, b_router, w_up, b_up, w_down, b_down):
    depth = w_in.shape[0]
    assert depth == 1, "single-layer step"
    nb, seq, d = x_prompt.shape
    db, dec_seq, _ = x_sample.shape
    n_pool, page, kv_heads, dh = cache_k.shape[1:]
    assert dh == ATT_HEAD_DIM
    att_heads = rel_bias.shape[1]
    assert att_heads == kv_heads * ATT_GROUP
    gdn_heads = a_log.shape[1]
    n_pages = page_table.shape[1]
    past = n_pages * page
    q_w = att_heads * dh
    kv_w = kv_heads * dh
    gdn_cw = conv_w.shape[2]
    z_w = gdn_heads * GDN_DK
    assert gdn_cw == 3 * z_w
    in_w = q_w + 2 * kv_w + gdn_cw + z_w + 2 * gdn_heads
    assert w_in.shape[2] == in_w and in_w - 2 * gdn_heads == (in_w // LANES) * LANES
    assert seq % MOBA_BLOCK == 0 and past % MOBA_BLOCK == 0 and MOBA_BLOCK % page == 0
    assert dec_seq <= MOBA_BLOCK and dec_seq >= GDN_CONV - 1 and dec_seq <= SUBLANES
    n_exp = w_router.shape[2]
    nblk = seq // MOBA_BLOCK
    widths = (q_w, kv_w, gdn_cw, z_w)

    n_mod = nb + db
    n_mod_pad = -(-n_mod // SUBLANES) * SUBLANES
    c_all = jnp.pad(jnp.concatenate([c_prompt, c_sample], axis=0), ((0, n_mod_pad - n_mod), (0, 0)))
    mod = _ada(c_all, w_ada[0], b_ada[0])
    mod_p = [mod[:nb, i * d:(i + 1) * d].reshape(nb, 1, d) for i in range(6)]
    m_s = db * dec_seq
    mod_s = [jnp.broadcast_to(mod[nb:n_mod, None, i * d:(i + 1) * d], (db, dec_seq, d)).reshape(1, m_s, d)
             for i in range(6)]

    w_in_pad = jnp.pad(w_in[0], ((0, 0), (0, LANES - 2 * gdn_heads))).astype(BF16)
    w_out_bf = w_out[0].astype(BF16)
    wr_pad = jnp.pad(w_router[0], ((0, 0), (0, LANES - n_exp)))
    br_pad = jnp.pad(b_router[0], (0, LANES - n_exp), constant_values=NEG).reshape(1, LANES)
    w_up_bf = w_up[0].astype(BF16)
    w_dn_bf = w_down[0].astype(BF16)

    tt_p = _pick_tile(seq, 512)
    q_p, k_p, v_p, conv_p, z_p, ba_p, kh_p, vt_p, km_p = _inproj(
        x_prompt, mod_p[0], mod_p[1], norm_attn_w[0], w_in_pad, widths, tt_p, True)
    km_h = km_p.reshape(nb, nblk, kv_heads, dh).transpose(0, 2, 1, 3)
    bias_t = _bias_tiles(rel_bias, nblk)
    attn_p = _attn_prompt(q_p, kh_p, vt_p, km_h, bias_t)
    og_p, ssm_p = _gdn_prompt(conv_p, ba_p, conv_w[0], a_log[0], dt_bias[0], gdn_heads, _pick_tile(seq, 256))
    x1_p, h2_p, topw_p, topi_p = _mid(x_prompt, attn_p, og_p, z_p, mod_p[2], mod_p[4], mod_p[3], gdn_norm_w[0],
                                      w_out_bf, norm_ffn_w[0], wr_pad, br_pad, tt_p, gdn_heads, True)
    m_p = nb * seq
    group = _pick_tile(m_p, MOE_GROUP)
    assert group % tt_p == 0 and seq % tt_p == 0
    moe_p = _moe_grouped(h2_p.reshape(m_p * (d // LANES), LANES), topw_p.reshape(m_p, LANES),
                         topi_p.reshape(m_p, LANES), w_up_bf, b_up[0], w_dn_bf, b_down[0], group)
    y_p = _moe_finish(moe_p, x1_p, mod_p[5], norm_final_w, group, tt_p)

    tt_s = _pick_tile(m_s, 512)
    q_s, k_s, v_s, conv_s, z_s, ba_s = _inproj(
        x_sample.reshape(1, m_s, d), mod_s[0], mod_s[1], norm_attn_w[0], w_in_pad, widths, tt_s, False)
    q4 = q_s.reshape(db, dec_seq, kv_heads, ATT_GROUP, dh).transpose(0, 2, 3, 1, 4)
    q4 = q4.reshape(db, kv_heads, ATT_GROUP * dec_seq, 1, dh)
    eye = jnp.eye(kv_heads, dtype=F32).reshape(1, kv_heads, 1, kv_heads, 1)
    qc = (q4 * eye).reshape(db, att_heads * dec_seq, kv_w)
    new_pad = ((0, 0), (0, SUBLANES - dec_seq), (0, 0))
    k_new = jnp.pad(k_s.reshape(db, dec_seq, kv_w), new_pad)
    v_new = jnp.pad(v_s.reshape(db, dec_seq, kv_w), new_pad)
    bias_r = _bias_rows(rel_bias, past, dec_seq, past + MOBA_BLOCK)
    ck_t = cache_k[0].transpose(0, 2, 3, 1).reshape(n_pool, kv_w, page)
    cv_t = cache_v[0].transpose(0, 2, 3, 1).reshape(n_pool, kv_w, page)
    attn_rows = _attn_sample(page_table, qc, k_new, v_new, bias_r, ck_t, cv_t, dec_seq, kv_heads)
    attn_s = attn_rows.reshape(db, att_heads, dec_seq, dh).transpose(0, 2, 1, 3).reshape(1, m_s, q_w)

    conv_s3 = conv_s.reshape(db, dec_seq, gdn_cw)
    ext = jnp.concatenate([state_conv[0], conv_s3], axis=1)
    gs = _pick_tile(db, SUBLANES)
    og_tm, ssm_s = _gdn_sample(jnp.swapaxes(ext, 0, 1), jnp.swapaxes(ba_s.reshape(db, dec_seq, LANES), 0, 1),
                               conv_w[0], a_log[0], dt_bias[0], state_ssm[0], gdn_heads, gs)
    og_s = jnp.swapaxes(og_tm, 0, 1).reshape(1, m_s, z_w)
    x1_s, h2_s, gates_s = _mid(x_sample.reshape(1, m_s, d), attn_s, og_s, z_s, mod_s[2], mod_s[4], mod_s[3],
                               gdn_norm_w[0], w_out_bf, norm_ffn_w[0], wr_pad, br_pad, tt_s, gdn_heads, False)
    y_s = _moe(h2_s, gates_s, x1_s, mod_s[5], w_up_bf, b_up[0], w_dn_bf, b_down[0], norm_final_w, tt_s)

    return (
        y_p,
        y_s.reshape(db, dec_seq, d),
        k_p.reshape(1, nb, seq, kv_heads, dh),
        v_p.reshape(1, nb, seq, kv_heads, dh),
        conv_p[:, seq - (GDN_CONV - 1):, :][None],
        ssm_p[None],
        k_s.reshape(1, db, dec_seq, kv_heads, dh),
        v_s.reshape(1, db, dec_seq, kv_heads, dh),
        ext[:, dec_seq:, :][None],
        ssm_s[None],
    )
```

```python
import functools
import math

import jax
import jax.numpy as jnp
from jax import lax
from jax.experimental import pallas as pl
from jax.experimental.pallas import tpu as pltpu

F32 = jnp.float32
BF16 = jnp.bfloat16
HIGHEST = lax.Precision.HIGHEST

ATT_HEAD_DIM = 64
ATT_GROUP = 2
MOBA_BLOCK = 256
MOBA_TOPK = 3
REL_BUCKETS = 32
REL_MAX_DIST = 4096
GDN_DK = 128
GDN_CONV = 4
GDN_CHUNK = 64
TOP_K = 4
SWIGLU_LIMIT = 7.0
SWIGLU_ALPHA = 1.702
EPS = 1e-6
NEG = -1e30

LANES = 128
SUBLANES = 8
VT_ROWS = ATT_HEAD_DIM + SUBLANES
LOG2E = math.log2(math.e)
VMEM_LIMIT = 48 * 1024 * 1024
MOE_VMEM_LIMIT = 56 * 1024 * 1024


def _cparams(sem):
    return pltpu.CompilerParams(dimension_semantics=sem, vmem_limit_bytes=VMEM_LIMIT)


def _dot(a, b, precision=None):
    return jnp.dot(a, b, precision=precision, preferred_element_type=F32)


def _dot_nt(a, b, precision=None):
    return lax.dot_general(a, b, (((1,), (1,)), ((), ())), precision=precision,
                           preferred_element_type=F32)


def _dot_tn(a, b, precision=None):
    return lax.dot_general(a, b, (((0,), (0,)), ((), ())), precision=precision,
                           preferred_element_type=F32)


def _split(x):
    hi = x.astype(BF16)
    return hi, (x - hi.astype(F32)).astype(BF16)


def _mm3(a, b):
    m = a.shape[0]
    ah, al = _split(a)
    bh, bl = _split(b)
    t = _dot(jnp.concatenate([ah, al], axis=0), bh)
    return t[:m] + t[m:] + _dot(ah, bl)


def _mm3_nt(a, b):
    m = a.shape[0]
    ah, al = _split(a)
    bh, bl = _split(b)
    t = _dot_nt(jnp.concatenate([ah, al], axis=0), bh)
    return t[:m] + t[m:] + _dot_nt(ah, bl)


def _silu(x):
    return x * jax.nn.sigmoid(x)


def _softplus(x):
    return jnp.maximum(x, 0.0) + jnp.log1p(jnp.exp(-jnp.abs(x)))


def _rel_bucket(dist):
    n = jnp.maximum(dist, 0)
    max_exact = REL_BUCKETS // 2
    nf = jnp.maximum(n, max_exact).astype(F32)
    large = max_exact + (jnp.log(nf / max_exact) / math.log(REL_MAX_DIST / max_exact)
                         * (REL_BUCKETS - max_exact)).astype(jnp.int32)
    return jnp.where(n < max_exact, n, jnp.minimum(large, REL_BUCKETS - 1))


def _ada_kernel(c_ref, w_ref, b_ref, o_ref):
    o_ref[...] = _dot(_silu(c_ref[...]), w_ref[...], HIGHEST) + b_ref[...]


def _ada(c, w_ada, b_ada):
    m, d = c.shape
    n = w_ada.shape[1]
    tn = d
    return pl.pallas_call(
        _ada_kernel,
        grid=(n // tn,),
        in_specs=[pl.BlockSpec((m, d), lambda j: (0, 0)),
                  pl.BlockSpec((d, tn), lambda j: (0, j)),
                  pl.BlockSpec((1, tn), lambda j: (0, j))],
        out_specs=pl.BlockSpec((m, tn), lambda j: (0, j)),
        out_shape=jax.ShapeDtypeStruct((m, n), F32),
        compiler_params=_cparams(("parallel",)),
        name="ada_modulation",
    )(c, w_ada, b_ada.reshape(1, n))


def _inproj_kernel(x_ref, sh_ref, sc_ref, nw_ref, w_ref, *out_refs, offs, heads_out, kv_heads):
    x = x_ref[0]
    xn = x * lax.rsqrt(jnp.mean(x * x, axis=-1, keepdims=True) + EPS) * nw_ref[...]
    h = (xn * (1.0 + sc_ref[0]) + sh_ref[0]).astype(BF16)
    o0, o1, o2, o3, o4, o5 = offs

    def proj(lo, hi):
        return _dot(h, w_ref[:, lo:hi])

    q_ref, k_ref, v_ref, conv_ref, z_ref, ba_ref = out_refs[:6]
    q_ref[0] = proj(o0, o1)
    k = proj(o1, o2)
    v = proj(o2, o3)
    k_ref[0] = k
    v_ref[0] = v
    conv_ref[0] = proj(o3, o4)
    z_ref[0] = proj(o4, o5)
    ba_ref[0] = proj(o5, o5 + LANES)
    if heads_out:
        kh_ref, vt_ref, km_ref = out_refs[6:]
        vt = v.T
        for b in range(x.shape[0] // MOBA_BLOCK):
            rows = slice(b * MOBA_BLOCK, (b + 1) * MOBA_BLOCK)
            kb = k[rows]
            km_ref[0, b] = jnp.mean(kb, axis=0, keepdims=True)
            for hh in range(kv_heads):
                cols = slice(hh * ATT_HEAD_DIM, (hh + 1) * ATT_HEAD_DIM)
                kh_ref[0, hh, b] = kb[:, cols].astype(BF16)
                vt_ref[0, hh, b, 0:ATT_HEAD_DIM, :] = vt[cols, rows].astype(BF16)
                vt_ref[0, hh, b, ATT_HEAD_DIM:VT_ROWS, :] = jnp.ones((VT_ROWS - ATT_HEAD_DIM, MOBA_BLOCK), BF16)


def _inproj(x, sh, sc, norm_w, w_pad, widths, tt, heads_out):
    n, t, d = x.shape
    per_token = sh.shape[1] != 1
    q_w, kv_w, conv_w_, z_w = widths
    offs = (0, q_w, q_w + kv_w, q_w + 2 * kv_w, q_w + 2 * kv_w + conv_w_, q_w + 2 * kv_w + conv_w_ + z_w)
    kv_heads = kv_w // ATT_HEAD_DIM
    nb_t = tt // MOBA_BLOCK
    mod_spec = (pl.BlockSpec((1, tt, d), lambda i, j: (i, j, 0)) if per_token
                else pl.BlockSpec((1, 1, d), lambda i, j: (i, 0, 0)))

    def tok(width):
        return pl.BlockSpec((1, tt, width), lambda i, j: (i, j, 0))

    out_specs = [tok(q_w), tok(kv_w), tok(kv_w), tok(conv_w_), tok(z_w), tok(LANES)]
    out_shape = [jax.ShapeDtypeStruct((n, t, w), F32) for w in (q_w, kv_w, kv_w, conv_w_, z_w, LANES)]
    if heads_out:
        nblk = t // MOBA_BLOCK
        out_specs += [
            pl.BlockSpec((1, kv_heads, nb_t, MOBA_BLOCK, ATT_HEAD_DIM), lambda i, j: (i, 0, j, 0, 0)),
            pl.BlockSpec((1, kv_heads, nb_t, VT_ROWS, MOBA_BLOCK), lambda i, j: (i, 0, j, 0, 0)),
            pl.BlockSpec((1, nb_t, 1, kv_w), lambda i, j: (i, j, 0, 0)),
        ]
        out_shape += [
            jax.ShapeDtypeStruct((n, kv_heads, nblk, MOBA_BLOCK, ATT_HEAD_DIM), BF16),
            jax.ShapeDtypeStruct((n, kv_heads, nblk, VT_ROWS, MOBA_BLOCK), BF16),
            jax.ShapeDtypeStruct((n, nblk, 1, kv_w), F32),
        ]
    return pl.pallas_call(
        functools.partial(_inproj_kernel, offs=offs, heads_out=heads_out, kv_heads=kv_heads),
        grid=(n, t // tt),
        in_specs=[pl.BlockSpec((1, tt, d), lambda i, j: (i, j, 0)), mod_spec, mod_spec,
                  pl.BlockSpec((1, d), lambda i, j: (0, 0)),
                  pl.BlockSpec(w_pad.shape, lambda i, j: (0, 0))],
        out_specs=out_specs,
        out_shape=out_shape,
        compiler_params=_cparams(("parallel", "parallel")),
        name="in_proj_prompt" if heads_out else "in_proj_sample",
    )(x, sh, sc, norm_w.reshape(1, d), w_pad)


def _bias_tile_kernel(rb_ref, o_ref):
    kvh = pl.program_id(0)
    delta = pl.num_programs(1) - 1 - pl.program_id(1)
    key = lax.broadcasted_iota(jnp.int32, (MOBA_BLOCK, MOBA_BLOCK), 0)
    qry = lax.broadcasted_iota(jnp.int32, (MOBA_BLOCK, MOBA_BLOCK), 1)
    bucket = _rel_bucket(delta * MOBA_BLOCK + qry - key)
    for g in range(ATT_GROUP):
        acc = jnp.zeros((MOBA_BLOCK, MOBA_BLOCK), F32)
        for t in range(REL_BUCKETS):
            acc = jnp.where(bucket == t, rb_ref[t, kvh * ATT_GROUP + g], acc)
        o_ref[0, 0, :, g * MOBA_BLOCK:(g + 1) * MOBA_BLOCK] = acc * LOG2E


def _bias_tiles(rel_bias, nblk):
    kv_heads = rel_bias.shape[1] // ATT_GROUP
    width = ATT_GROUP * MOBA_BLOCK
    return pl.pallas_call(
        _bias_tile_kernel,
        grid=(kv_heads, nblk),
        in_specs=[pl.BlockSpec(memory_space=pltpu.SMEM)],
        out_specs=pl.BlockSpec((1, 1, MOBA_BLOCK, width), lambda h, dlt: (h, dlt, 0, 0)),
        out_shape=jax.ShapeDtypeStruct((kv_heads, nblk, MOBA_BLOCK, width), F32),
        compiler_params=_cparams(("parallel", "parallel")),
        name="rel_bias_tiles",
    )(rel_bias)


def _bias_rows_kernel(tab_ref, o_ref, *, past, dec_seq):
    rows, width = o_ref.shape
    row = lax.broadcasted_iota(jnp.int32, (rows, width), 0)
    key = lax.broadcasted_iota(jnp.int32, (rows, width), 1)
    bucket = _rel_bucket(past + lax.rem(row, dec_seq) - key)
    acc = jnp.zeros((rows, width), F32)
    for t in range(REL_BUCKETS):
        acc = jnp.where(bucket == t, tab_ref[:, t:t + 1], acc)
    o_ref[...] = acc


def _bias_rows(rel_bias, past, dec_seq, width):
    heads = rel_bias.shape[1]
    rows = heads * dec_seq
    tab = jnp.repeat(rel_bias.T, dec_seq, axis=0)
    return pl.pallas_call(
        functools.partial(_bias_rows_kernel, past=past, dec_seq=dec_seq),
        out_shape=jax.ShapeDtypeStruct((rows, width), F32),
        name="rel_bias_rows",
    )(tab)


def _topk_rank_rows(gm, idx, n):
    rank = jnp.zeros(gm.shape, jnp.int32)
    for m in range(n):
        row = gm[m:m + 1, :]
        beats = (row > gm) | ((row == gm) & (m < idx))
        rank = rank + beats.astype(jnp.int32)
    return rank


def _attn_prompt_kernel(q_ref, kh_ref, vt_ref, km_ref, bias_ref, o_ref, sel_ref, raw_ref, *, nblk):
    qt = pl.program_id(2)
    q_t = q_ref[0].T
    tq = q_t.shape[1]
    width = ATT_GROUP * tq
    q_all = jnp.concatenate([q_t[g * ATT_HEAD_DIM:(g + 1) * ATT_HEAD_DIM, :] for g in range(ATT_GROUP)], axis=1)
    blk = lax.broadcasted_iota(jnp.int32, (nblk, width), 0)
    past = blk < qt
    gate = _dot(km_ref[0, 0], q_all, HIGHEST)
    rank = _topk_rank_rows(jnp.where(past, gate, NEG), blk, nblk)
    sel_ref[...] = (past & (rank < MOBA_TOPK)).astype(F32)
    qs = (q_all * (ATT_HEAD_DIM ** -0.5 * LOG2E)).astype(BF16)
    key = lax.broadcasted_iota(jnp.int32, (MOBA_BLOCK, width), 0)
    qry = lax.rem(lax.broadcasted_iota(jnp.int32, (MOBA_BLOCK, width), 1), tq)

    own_tile = nblk - 1
    s = jnp.where(key <= qry, _dot(kh_ref[0, 0, qt], qs) + bias_ref[0, own_tile], NEG)
    m0 = jnp.max(s, axis=0, keepdims=True)
    carry = (m0, _dot(vt_ref[0, 0, qt], jnp.exp2(s - m0).astype(BF16)))

    def update(carry, s, vt):
        m, acc = carry
        m_new = jnp.maximum(m, jnp.max(s, axis=0, keepdims=True))
        return m_new, jnp.exp2(m - m_new) * acc + _dot(vt, jnp.exp2(s - m_new).astype(BF16))

    def single(carry):
        s = _dot(kh_ref[0, 0, 0], qs) + bias_ref[0, own_tile - qt]
        return update(carry, jnp.where(sel_ref[0:1, :] > 0.5, s, NEG), vt_ref[0, 0, 0])

    odd = lax.rem(qt, 2)
    carry = lax.cond(odd == 1, single, lambda c: c, carry)

    n_pairs = qt // 2

    def raw_scores(i):
        kb = jnp.minimum(odd + 2 * jnp.minimum(i, n_pairs - 1), nblk - 2)
        kb = jnp.maximum(kb, 0)
        return _dot(kh_ref[0, 0, pl.ds(kb, 2)].reshape(2 * MOBA_BLOCK, ATT_HEAD_DIM), qs)

    raw_ref[...] = raw_scores(0)

    def pair(i, carry):
        kb = odd + 2 * i
        s = raw_ref[...] + bias_ref[0, pl.ds(own_tile - qt + kb, 2)].reshape(2 * MOBA_BLOCK, width)
        raw_ref[...] = raw_scores(i + 1)
        s = jnp.concatenate([jnp.where(sel_ref[pl.ds(kb, 1), :] > 0.5, s[:MOBA_BLOCK], NEG),
                             jnp.where(sel_ref[pl.ds(kb + 1, 1), :] > 0.5, s[MOBA_BLOCK:], NEG)], axis=0)
        vt2 = jnp.concatenate([vt_ref[0, 0, kb], vt_ref[0, 0, kb + 1]], axis=1)
        return update(carry, s, vt2)

    _, acc = lax.fori_loop(0, n_pairs, pair, carry)
    out_t = acc[:ATT_HEAD_DIM] / acc[ATT_HEAD_DIM:ATT_HEAD_DIM + 1]
    o_ref[0] = jnp.concatenate([out_t[:, g * tq:(g + 1) * tq].T for g in range(ATT_GROUP)], axis=1)


def _attn_prompt(q, kh, vt, km, bias):
    n, t, q_w = q.shape
    kv_heads, nblk = kh.shape[1], kh.shape[2]
    assert nblk >= 2, "the look-ahead in the block-pair loop reads two key blocks"
    gw = ATT_GROUP * ATT_HEAD_DIM
    return pl.pallas_call(
        functools.partial(_attn_prompt_kernel, nblk=nblk),
        grid=(n, kv_heads, nblk),
        in_specs=[
            pl.BlockSpec((1, MOBA_BLOCK, gw), lambda i, h, j: (i, j, h)),
            pl.BlockSpec((1, 1, nblk, MOBA_BLOCK, ATT_HEAD_DIM), lambda i, h, j: (i, h, 0, 0, 0)),
            pl.BlockSpec((1, 1, nblk, VT_ROWS, MOBA_BLOCK), lambda i, h, j: (i, h, 0, 0, 0)),
            pl.BlockSpec((1, 1, nblk, ATT_HEAD_DIM), lambda i, h, j: (i, h, 0, 0)),
            pl.BlockSpec((1, nblk, MOBA_BLOCK, ATT_GROUP * MOBA_BLOCK), lambda i, h, j: (h, 0, 0, 0)),
        ],
        out_specs=pl.BlockSpec((1, MOBA_BLOCK, gw), lambda i, h, j: (i, j, h)),
        out_shape=jax.ShapeDtypeStruct((n, t, q_w), F32),
        scratch_shapes=[pltpu.VMEM((nblk, ATT_GROUP * MOBA_BLOCK), F32),
                        pltpu.VMEM((2 * MOBA_BLOCK, ATT_GROUP * MOBA_BLOCK), F32)],
        compiler_params=_cparams(("parallel", "parallel", "arbitrary")),
        name="moba_prompt",
    )(q, kh, vt, km, bias)


def _attn_sample_kernel(pt_ref, qc_ref, knew_ref, vnew_ref, bias_ref, ck_hbm, cv_hbm, o_ref,
                        kbuf, vbuf, sem, *, n_pages, dec_seq, kv_heads):
    s = pl.program_id(0)
    nseq = pl.num_programs(0)
    slot = lax.rem(s, 2)
    page = kbuf.shape[3]
    pages_per_blk = MOBA_BLOCK // page
    nblk = n_pages // pages_per_blk
    past_len = nblk * MOBA_BLOCK

    def start_fetch(seq, sl):
        def one(p, carry):
            pg = pt_ref[seq, p]
            pltpu.make_async_copy(ck_hbm.at[pg], kbuf.at[sl, p], sem.at[0, sl]).start()
            pltpu.make_async_copy(cv_hbm.at[pg], vbuf.at[sl, p], sem.at[1, sl]).start()
            return carry
        lax.fori_loop(0, n_pages, one, 0)

    @pl.when(s == 0)
    def _():
        start_fetch(0, 0)

    @pl.when(s + 1 < nseq)
    def _():
        start_fetch(s + 1, 1 - slot)

    pltpu.make_async_copy(ck_hbm.at[pl.ds(0, n_pages)], kbuf.at[slot], sem.at[0, slot]).wait()
    pltpu.make_async_copy(cv_hbm.at[pl.ds(0, n_pages)], vbuf.at[slot], sem.at[1, slot]).wait()

    qc = qc_ref[0]
    rows, kv_w = qc.shape
    scale = ATT_HEAD_DIM ** -0.5
    q_hi, q_lo = _split(qc)
    q_both = jnp.concatenate([q_hi, q_lo], axis=0)

    s_pages = []
    for p in range(n_pages):
        k_hi, k_lo = _split(kbuf[slot, p])
        t = _dot(q_both, k_hi)
        s_pages.append(t[:rows] + t[rows:] + _dot(q_hi, k_lo))

    lane = lax.broadcasted_iota(jnp.int32, (rows, LANES), 1)
    gate = jnp.zeros((rows, LANES), F32)
    for b in range(nblk):
        tot = s_pages[b * pages_per_blk]
        for p in range(b * pages_per_blk + 1, (b + 1) * pages_per_blk):
            tot = tot + s_pages[p]
        gate = jnp.where(lane == b, jnp.sum(tot, axis=1, keepdims=True) * (1.0 / MOBA_BLOCK), gate)
    rank = jnp.zeros((rows, LANES), jnp.int32)
    for m in range(nblk):
        col = gate[:, m:m + 1]
        rank = rank + ((col > gate) | ((col == gate) & (m < lane))).astype(jnp.int32)
    sel = rank < MOBA_TOPK

    masked = []
    for p in range(n_pages):
        b = p // pages_per_blk
        sp = s_pages[p] * scale + bias_ref[:, p * page:(p + 1) * page]
        masked.append(jnp.where(sel[:, b:b + 1], sp, NEG))
    n_new = knew_ref.shape[1]
    r_idx = lax.rem(lax.broadcasted_iota(jnp.int32, (rows, n_new), 0), dec_seq)
    j_idx = lax.broadcasted_iota(jnp.int32, (rows, n_new), 1)
    s_own = _dot_nt(q_hi, knew_ref[0].astype(BF16)) * scale + bias_ref[:, past_len:past_len + n_new]
    s_own = jnp.where(j_idx <= r_idx, s_own, NEG)
    m = jnp.max(s_own, axis=1, keepdims=True)
    for sp in masked:
        m = jnp.maximum(m, jnp.max(sp, axis=1, keepdims=True))
    p_own = jnp.exp(s_own - m)
    l = jnp.sum(p_own, axis=1, keepdims=True)
    acc = _dot(p_own.astype(BF16), vnew_ref[0].astype(BF16))
    for p, sp in enumerate(masked):
        prob = jnp.exp(sp - m)
        l = l + jnp.sum(prob, axis=1, keepdims=True)
        acc = acc + _dot_nt(prob.astype(BF16), vbuf[slot, p].astype(BF16))
    out = acc / l
    rpk = rows // kv_heads
    for h in range(kv_heads):
        o_ref[0, h * rpk:(h + 1) * rpk, :] = out[h * rpk:(h + 1) * rpk,
                                                 h * ATT_HEAD_DIM:(h + 1) * ATT_HEAD_DIM]


def _attn_sample(page_table, qc, k_new, v_new, bias, cache_k, cache_v, dec_seq, kv_heads):
    b, rows, kv_w = qc.shape
    n_pages = page_table.shape[1]
    page = cache_k.shape[2]
    n_new = k_new.shape[1]
    grid_spec = pltpu.PrefetchScalarGridSpec(
        num_scalar_prefetch=1,
        grid=(b,),
        in_specs=[
            pl.BlockSpec((1, rows, kv_w), lambda i, pt: (i, 0, 0)),
            pl.BlockSpec((1, n_new, kv_w), lambda i, pt: (i, 0, 0)),
            pl.BlockSpec((1, n_new, kv_w), lambda i, pt: (i, 0, 0)),
            pl.BlockSpec(bias.shape, lambda i, pt: (0, 0)),
            pl.BlockSpec(memory_space=pl.ANY),
            pl.BlockSpec(memory_space=pl.ANY),
        ],
        out_specs=pl.BlockSpec((1, rows, ATT_HEAD_DIM), lambda i, pt: (i, 0, 0)),
        scratch_shapes=[pltpu.VMEM((2, n_pages, kv_w, page), F32),
                        pltpu.VMEM((2, n_pages, kv_w, page), F32),
                        pltpu.SemaphoreType.DMA((2, 2))],
    )
    return pl.pallas_call(
        functools.partial(_attn_sample_kernel, n_pages=n_pages, dec_seq=dec_seq, kv_heads=kv_heads),
        grid_spec=grid_spec,
        out_shape=jax.ShapeDtypeStruct((b, rows, ATT_HEAD_DIM), F32),
        compiler_params=_cparams(("arbitrary",)),
        name="moba_sample",
    )(page_table, qc, k_new, v_new, bias, cache_k, cache_v)


def _gdn_gates(ba, alog, dt):
    return jax.nn.sigmoid(ba), -jnp.exp(alog) * _softplus(ba + dt)


def _gdn_prompt_kernel(x_ref, ba_ref, bat_ref, cw_ref, alog_ref, dt_ref, alogt_ref, dtt_ref,
                       o_ref, s_ref, ext_ref, *, heads):
    ct, cw = x_ref.shape[1], x_ref.shape[2]
    qk_w = heads * GDN_DK
    cs = GDN_CHUNK
    halo = SUBLANES

    @pl.when(pl.program_id(1) == 0)
    def _():
        ext_ref[0:halo, :] = jnp.zeros((halo, cw), F32)
        s_ref[...] = jnp.zeros(s_ref.shape, F32)

    x = x_ref[0]
    ext_ref[halo:halo + ct, :] = x
    w = cw_ref[...]
    conv = x * w[GDN_CONV - 1:GDN_CONV, :]
    for i in range(GDN_CONV - 1):
        conv = conv + ext_ref[pl.ds(halo - (GDN_CONV - 1) + i, ct), :] * w[i:i + 1, :]
    ext_ref[0:halo, :] = x[ct - halo:ct, :]
    act = _silu(conv)

    beta_f, g_f = _gdn_gates(ba_ref[0], alog_ref[...], dt_ref[...])
    _, g_t = _gdn_gates(bat_ref[0], alogt_ref[...], dtt_ref[...])

    r_i = lax.broadcasted_iota(jnp.int32, (cs, cs), 0)
    c_i = lax.broadcasted_iota(jnp.int32, (cs, cs), 1)
    tril = r_i >= c_i
    strict = r_i > c_i
    eye = (r_i == c_i).astype(F32)
    cum_l = tril.astype(F32)
    cum_u = (r_i <= c_i).astype(F32)

    qn, kn, vv = [], [], []
    for h in range(heads):
        qh = act[:, h * GDN_DK:(h + 1) * GDN_DK]
        kh = act[:, qk_w + h * GDN_DK:qk_w + (h + 1) * GDN_DK]
        qn.append(qh * lax.rsqrt(jnp.sum(qh * qh, axis=-1, keepdims=True) + EPS) * (GDN_DK ** -0.5))
        kn.append(kh * lax.rsqrt(jnp.sum(kh * kh, axis=-1, keepdims=True) + EPS))
        vv.append(act[:, 2 * qk_w + h * GDN_DK:2 * qk_w + (h + 1) * GDN_DK])

    n_chunks = ct // cs
    units = [(c, h) for c in range(n_chunks) for h in range(heads)]
    gc, gct = [], []
    for c in range(n_chunks):
        rows = slice(c * cs, (c + 1) * cs)
        gc.append(_dot(cum_l, g_f[rows, :], HIGHEST))
        gct.append(_dot(g_t[:, rows], cum_u, HIGHEST))
    q_u, k_u, kbeta_u, vbeta_u, gcc_u, egc_u, decay_u = {}, {}, {}, {}, {}, {}, {}
    for (c, h) in units:
        rows = slice(c * cs, (c + 1) * cs)
        beta = beta_f[rows, h:h + 1]
        gcc = gc[c][:, heads + h:heads + h + 1]
        gcr = gct[c][heads + h:heads + h + 1, :]
        q_u[c, h], k_u[c, h] = qn[h][rows], kn[h][rows]
        kbeta_u[c, h] = k_u[c, h] * beta
        vbeta_u[c, h] = vv[h][rows] * beta
        gcc_u[c, h], egc_u[c, h] = gcc, jnp.exp(gcc)
        decay_u[c, h] = jnp.where(tril, jnp.exp(jnp.where(tril, gcc - gcr, 0.0)), 0.0)
    a_u = {u: _mm3_nt(jnp.concatenate([kbeta_u[u], q_u[u]], axis=0), k_u[u]) for u in units}
    x_u = {u: -jnp.where(strict, a_u[u][:cs] * decay_u[u], 0.0) for u in units}
    intra_u = {u: jnp.where(tril, a_u[u][cs:] * decay_u[u], 0.0) for u in units}
    p_u = {u: eye + x_u[u] for u in units}
    xp_u = {u: _mm3(x_u[u], x_u[u]) for u in units}
    npow = 2
    while npow * 2 < cs:
        pr_u = {u: _mm3(jnp.concatenate([p_u[u], xp_u[u]], axis=0), xp_u[u]) for u in units}
        p_u = {u: p_u[u] + pr_u[u][:cs] for u in units}
        xp_u = {u: pr_u[u][cs:] for u in units}
        npow *= 2
    p_u = {u: p_u[u] + _mm3(p_u[u], xp_u[u]) for u in units}
    uw_u = {u: _dot(p_u[u].astype(BF16),
                    jnp.concatenate([vbeta_u[u], kbeta_u[u] * egc_u[u]], axis=1).astype(BF16)) for u in units}

    state = [s_ref[0, h] for h in range(heads)]
    for c in range(n_chunks):
        rows = slice(c * cs, (c + 1) * cs)
        ws = [_dot(jnp.concatenate([uw_u[c, h][:, GDN_DK:], q_u[c, h] * egc_u[c, h]], axis=0).astype(BF16),
                   state[h].astype(BF16)) for h in range(heads)]
        v_new = [uw_u[c, h][:, :GDN_DK] - ws[h][:cs] for h in range(heads)]
        v_bf = [v.astype(BF16) for v in v_new]
        for h in range(heads):
            o_ref[0, rows, h * GDN_DK:(h + 1) * GDN_DK] = ws[h][cs:] + _dot(intra_u[c, h].astype(BF16), v_bf[h])
        for h in range(heads):
            gcc = gcc_u[c, h]
            g_last = gcc[cs - 1:cs, :]
            state[h] = state[h] * jnp.exp(g_last) + _dot_tn(
                (k_u[c, h] * jnp.exp(g_last - gcc)).astype(BF16), v_bf[h])
    for h in range(heads):
        s_ref[0, h] = state[h]


def _gdn_prompt(conv_pre, ba, conv_w, a_log, dt_bias, heads, ct):
    n, t, cw = conv_pre.shape
    bat = jnp.swapaxes(ba[:, :, :2 * heads], 1, 2)
    pad = jnp.zeros((heads,), F32)
    alog = jnp.concatenate([pad, a_log])
    dt = jnp.concatenate([pad, dt_bias])
    lane_pad = (0, LANES - 2 * heads)
    return pl.pallas_call(
        functools.partial(_gdn_prompt_kernel, heads=heads),
        grid=(n, t // ct),
        in_specs=[
            pl.BlockSpec((1, ct, cw), lambda i, j: (i, j, 0)),
            pl.BlockSpec((1, ct, LANES), lambda i, j: (i, j, 0)),
            pl.BlockSpec((1, 2 * heads, ct), lambda i, j: (i, 0, j)),
            pl.BlockSpec(conv_w.shape, lambda i, j: (0, 0)),
            pl.BlockSpec((1, LANES), lambda i, j: (0, 0)),
            pl.BlockSpec((1, LANES), lambda i, j: (0, 0)),
            pl.BlockSpec((2 * heads, 1), lambda i, j: (0, 0)),
            pl.BlockSpec((2 * heads, 1), lambda i, j: (0, 0)),
        ],
        out_specs=[pl.BlockSpec((1, ct, heads * GDN_DK), lambda i, j: (i, j, 0)),
                   pl.BlockSpec((1, heads, GDN_DK, GDN_DK), lambda i, j: (i, 0, 0, 0))],
        out_shape=[jax.ShapeDtypeStruct((n, t, heads * GDN_DK), F32),
                   jax.ShapeDtypeStruct((n, heads, GDN_DK, GDN_DK), F32)],
        scratch_shapes=[pltpu.VMEM((SUBLANES + ct, cw), F32)],
        compiler_params=_cparams(("parallel", "arbitrary")),
        name="gdn_prompt",
    )(conv_pre, ba, bat, conv_w, jnp.pad(alog, lane_pad).reshape(1, LANES),
      jnp.pad(dt, lane_pad).reshape(1, LANES), alog.reshape(2 * heads, 1), dt.reshape(2 * heads, 1))


def _gdn_sample_kernel(ext_ref, ba_ref, cw_ref, alog_ref, dt_ref, s_in_ref, o_ref, s_out_ref, *, heads):
    t_len = ba_ref.shape[0]
    gs = ext_ref.shape[1]
    qk_w = heads * GDN_DK
    w = cw_ref[...]
    q_t, k_t, v_r, beta_r, dec_r = [], [], [], [], []
    for t in range(t_len):
        conv = ext_ref[t] * w[0:1, :]
        for i in range(1, GDN_CONV):
            conv = conv + ext_ref[t + i] * w[i:i + 1, :]
        act = _silu(conv)
        beta_f, g_f = _gdn_gates(ba_ref[t], alog_ref[...], dt_ref[...])
        beta_r.append(beta_f)
        dec_r.append(jnp.exp(g_f))
        qs, ks, vs = [], [], []
        for h in range(heads):
            qh = act[:, h * GDN_DK:(h + 1) * GDN_DK]
            kh = act[:, qk_w + h * GDN_DK:qk_w + (h + 1) * GDN_DK]
            qn = qh * lax.rsqrt(jnp.sum(qh * qh, axis=-1, keepdims=True) + EPS) * (GDN_DK ** -0.5)
            kn = kh * lax.rsqrt(jnp.sum(kh * kh, axis=-1, keepdims=True) + EPS)
            qs.append(qn.T)
            ks.append(kn.T)
            vs.append(act[:, 2 * qk_w + h * GDN_DK:2 * qk_w + (h + 1) * GDN_DK])
        q_t.append(qs)
        k_t.append(ks)
        v_r.append(vs)
    for i in range(gs):
        for h in range(heads):
            st = s_in_ref[i, h]
            for t in range(t_len):
                st = st * dec_r[t][i:i + 1, heads + h:heads + h + 1]
                kcol = k_t[t][h][:, i:i + 1]
                ks_row = jnp.sum(st * kcol, axis=0, keepdims=True)
                delta = (v_r[t][h][i:i + 1, :] - ks_row) * beta_r[t][i:i + 1, h:h + 1]
                st = st + kcol * delta
                o_ref[t, i:i + 1, h * GDN_DK:(h + 1) * GDN_DK] = jnp.sum(
                    st * q_t[t][h][:, i:i + 1], axis=0, keepdims=True)
            s_out_ref[i, h] = st


def _gdn_sample(ext_tm, ba_tm, conv_w, a_log, dt_bias, state, heads, gs):
    t_len, b, _ = ba_tm.shape
    cw = ext_tm.shape[2]
    pad = jnp.zeros((heads,), F32)
    lane_pad = (0, LANES - 2 * heads)
    alog = jnp.pad(jnp.concatenate([pad, a_log]), lane_pad).reshape(1, LANES)
    dt = jnp.pad(jnp.concatenate([pad, dt_bias]), lane_pad).reshape(1, LANES)
    return pl.pallas_call(
        functools.partial(_gdn_sample_kernel, heads=heads),
        grid=(b // gs,),
        in_specs=[
            pl.BlockSpec((ext_tm.shape[0], gs, cw), lambda i: (0, i, 0)),
            pl.BlockSpec((t_len, gs, LANES), lambda i: (0, i, 0)),
            pl.BlockSpec(conv_w.shape, lambda i: (0, 0)),
            pl.BlockSpec((1, LANES), lambda i: (0, 0)),
            pl.BlockSpec((1, LANES), lambda i: (0, 0)),
            pl.BlockSpec((gs, heads, GDN_DK, GDN_DK), lambda i: (i, 0, 0, 0)),
        ],
        out_specs=[pl.BlockSpec((t_len, gs, heads * GDN_DK), lambda i: (0, i, 0)),
                   pl.BlockSpec((gs, heads, GDN_DK, GDN_DK), lambda i: (i, 0, 0, 0))],
        out_shape=[jax.ShapeDtypeStruct((t_len, b, heads * GDN_DK), F32),
                   jax.ShapeDtypeStruct(state.shape, F32)],
        compiler_params=_cparams(("parallel",)),
        name="gdn_sample",
    )(ext_tm, ba_tm, conv_w, alog, dt, state)


def _mid_kernel(x_ref, attn_ref, og_ref, z_ref, gta_ref, scf_ref, shf_ref, gw_ref, wout_ref, nf_ref,
                wr_ref, br_ref, x1_ref, h2_ref, gates_ref, topi_ref=None, *, heads, grouped):
    og = og_ref[0]
    z = z_ref[0]
    parts = [attn_ref[0]]
    for h in range(heads):
        o = og[:, h * GDN_DK:(h + 1) * GDN_DK]
        y = o * lax.rsqrt(jnp.mean(o * o, axis=-1, keepdims=True) + EPS) * gw_ref[...]
        parts.append(y * _silu(z[:, h * GDN_DK:(h + 1) * GDN_DK]))
    mix = jnp.concatenate(parts, axis=1).astype(BF16)
    x1 = x_ref[0] + gta_ref[0] * _dot(mix, wout_ref[...])
    x1_ref[0] = x1
    h2 = (x1 * lax.rsqrt(jnp.mean(x1 * x1, axis=-1, keepdims=True) + EPS) * nf_ref[...]
          * (1.0 + scf_ref[0]) + shf_ref[0])
    logits = _dot(h2, wr_ref[...], HIGHEST) + br_ref[...]
    lane = lax.broadcasted_iota(jnp.int32, logits.shape, 1)
    work = logits
    sel = jnp.zeros(logits.shape, jnp.bool_)
    vals, idxs = [], []
    for j in range(TOP_K):
        mx = jnp.max(work, axis=-1, keepdims=True)
        idx = jnp.min(jnp.where(work == mx, lane, LANES), axis=-1, keepdims=True)
        pick = lane == idx
        vals.append(mx)
        idxs.append(idx)
        sel = sel | pick
        work = jnp.where(pick, -jnp.inf, work)
    if grouped:
        tt = h2.shape[0]
        for s in range(h2.shape[1] // LANES):
            h2_ref[0, pl.ds(s, tt, stride=h2.shape[1] // LANES), :] = h2[:, s * LANES:(s + 1) * LANES]
        ex = [jnp.exp(v - vals[0]) for v in vals]
        denom = ex[0]
        for v in ex[1:]:
            denom = denom + v
        topi = jnp.zeros(logits.shape, jnp.int32)
        topw = jnp.zeros(logits.shape, F32)
        for j in range(TOP_K):
            topi = jnp.where(lane == j, idxs[j], topi)
            topw = jnp.where(lane == j, ex[j] / denom, topw)
        gates_ref[0] = topw
        topi_ref[0] = topi
    else:
        h2_ref[0] = h2.astype(BF16)
        e = jnp.where(sel, jnp.exp(logits - vals[0]), 0.0)
        gates_ref[0] = e / jnp.sum(e, axis=-1, keepdims=True)


def _mid(x, attn, og, z, gta, scf, shf, gdn_norm_w, w_out_bf, norm_ffn_w, wr_pad, br_pad, tt, heads, grouped):
    n, t, d = x.shape
    slab = d // LANES
    per_token = gta.shape[1] != 1
    mod_spec = (pl.BlockSpec((1, tt, d), lambda i, j: (i, j, 0)) if per_token
                else pl.BlockSpec((1, 1, d), lambda i, j: (i, 0, 0)))

    def tok(width):
        return pl.BlockSpec((1, tt, width), lambda i, j: (i, j, 0))

    def full(a):
        return pl.BlockSpec(a.shape, lambda i, j: (0,) * a.ndim)

    gw = gdn_norm_w.reshape(1, -1)
    nf = norm_ffn_w.reshape(1, d)
    if grouped:
        out_specs = [tok(d), pl.BlockSpec((1, tt * slab, LANES), lambda i, j: (i, j, 0)), tok(LANES), tok(LANES)]
        out_shape = [jax.ShapeDtypeStruct((n, t, d), F32), jax.ShapeDtypeStruct((n, t * slab, LANES), F32),
                     jax.ShapeDtypeStruct((n, t, LANES), F32), jax.ShapeDtypeStruct((n, t, LANES), jnp.int32)]
    else:
        out_specs = [tok(d), tok(d), tok(LANES)]
        out_shape = [jax.ShapeDtypeStruct((n, t, d), F32), jax.ShapeDtypeStruct((n, t, d), BF16),
                     jax.ShapeDtypeStruct((n, t, LANES), F32)]
    return pl.pallas_call(
        functools.partial(_mid_kernel, heads=heads, grouped=grouped),
        grid=(n, t // tt),
        in_specs=[tok(d), tok(attn.shape[2]), tok(og.shape[2]), tok(z.shape[2]), mod_spec, mod_spec, mod_spec,
                  full(gw), full(w_out_bf), full(nf), full(wr_pad), full(br_pad)],
        out_specs=out_specs,
        out_shape=out_shape,
        compiler_params=_cparams(("parallel", "parallel")),
        name="out_proj_router_grouped" if grouped else "out_proj_router",
    )(x, attn, og, z, gta, scf, shf, gw, w_out_bf, nf, wr_pad, br_pad)


def _expert_ffn(h, wup_ref, bup_ref, wdn_ref, bdn_ref, ff_chunk):
    d_ff = wdn_ref.shape[1]
    y = jnp.zeros((h.shape[0], wdn_ref.shape[2]), F32)
    for c in range(d_ff // ff_chunk):
        lo, hi = c * ff_chunk, (c + 1) * ff_chunk
        x_glu = _dot(h, wup_ref[0, :, lo:hi]) + bup_ref[0, :, lo:hi]
        x_lin = _dot(h, wup_ref[0, :, d_ff + lo:d_ff + hi]) + bup_ref[0, :, d_ff + lo:d_ff + hi]
        x_glu = jnp.minimum(x_glu, SWIGLU_LIMIT)
        x_lin = jnp.clip(x_lin, -SWIGLU_LIMIT, SWIGLU_LIMIT)
        act = x_glu * jax.nn.sigmoid(SWIGLU_ALPHA * x_glu) * (x_lin + 1.0)
        y = y + _dot(act.astype(BF16), wdn_ref[0, lo:hi, :])
    return y + bdn_ref[0]


def _moe_kernel(h2_ref, gates_ref, x1_ref, gtf_ref, wup_ref, bup_ref, wdn_ref, bdn_ref, nw_ref,
                y_ref, acc_ref, *, ff_chunk):
    e = pl.program_id(2)
    n_exp = pl.num_programs(2)

    @pl.when(e == 0)
    def _():
        acc_ref[...] = jnp.zeros(acc_ref.shape, F32)

    gates = gates_ref[0]
    lane = lax.broadcasted_iota(jnp.int32, gates.shape, 1)
    gcol = jnp.sum(jnp.where(lane == e, gates, 0.0), axis=-1, keepdims=True)
    acc_ref[...] += gcol * _expert_ffn(h2_ref[0], wup_ref, bup_ref, wdn_ref, bdn_ref, ff_chunk)

    @pl.when(e == n_exp - 1)
    def _():
        x = x1_ref[0] + gtf_ref[0] * acc_ref[...]
        y_ref[0] = x * lax.rsqrt(jnp.mean(x * x, axis=-1, keepdims=True) + EPS) * nw_ref[...]


def _moe(h2, gates, x1, gtf, w_up_bf, b_up, w_dn_bf, b_dn, norm_final_w, tm):
    n, t, d = x1.shape
    n_exp, _, ff2 = w_up_bf.shape
    d_ff = ff2 // 2
    per_token = gtf.shape[1] != 1
    mod_spec = (pl.BlockSpec((1, tm, d), lambda i, j, e: (i, j, 0)) if per_token
                else pl.BlockSpec((1, 1, d), lambda i, j, e: (i, 0, 0)))

    def tok(width):
        return pl.BlockSpec((1, tm, width), lambda i, j, e: (i, j, 0))

    return pl.pallas_call(
        functools.partial(_moe_kernel, ff_chunk=min(512, d_ff)),
        grid=(n, t // tm, n_exp),
        in_specs=[tok(d), tok(LANES), tok(d), mod_spec,
                  pl.BlockSpec((1, d, ff2), lambda i, j, e: (e, 0, 0)),
                  pl.BlockSpec((1, 1, ff2), lambda i, j, e: (e, 0, 0)),
                  pl.BlockSpec((1, d_ff, d), lambda i, j, e: (e, 0, 0)),
                  pl.BlockSpec((1, 1, d), lambda i, j, e: (e, 0, 0)),
                  pl.BlockSpec((1, d), lambda i, j, e: (0, 0))],
        out_specs=tok(d),
        out_shape=jax.ShapeDtypeStruct((n, t, d), F32),
        scratch_shapes=[pltpu.VMEM((tm, d), F32)],
        compiler_params=_cparams(("parallel", "parallel", "arbitrary")),
        name="moe_ffn",
    )(h2, gates, x1, gtf, w_up_bf, b_up.reshape(n_exp, 1, ff2), w_dn_bf, b_dn.reshape(n_exp, 1, d),
      norm_final_w.reshape(1, d))


MOE_GROUP = 4096
MOE_CHUNK = 256
MOE_SCATTER_BATCH = 8


def _moe_grouped_kernel(cnt_ref, off_ref, tok_ref, wgt_ref, xg_ref, wup_ref, bup_ref, wdn_ref, bdn_ref,
                        y_ref, xbuf, obuf, *, group, chunk, ff_chunk):
    g = pl.program_id(0)
    e = pl.program_id(1)
    slab = xbuf.shape[0] // chunk

    @pl.when(e == 0)
    def _():
        y_ref[...] = jnp.zeros(y_ref.shape, F32)
        obuf[...] = jnp.zeros(obuf.shape, F32)

    seg = g * pl.num_programs(1) + e
    off = off_ref[seg]
    end = off + cnt_ref[seg]
    n_chunks = (end - off + chunk - 1) // chunk

    def gather(r0):
        for i in range(chunk):
            tok = tok_ref[0, 0, r0 + i]
            xbuf[pl.ds(i * slab, slab), :] = xg_ref[0, pl.ds(pl.multiple_of(tok * slab, slab), slab), :]

    def scatter(r0, live):
        lim = jnp.where(live, end, 0)
        for b0 in range(0, chunk, MOE_SCATTER_BATCH):
            rows, wgts = [], []
            for i in range(b0, b0 + MOE_SCATTER_BATCH):
                valid = r0 + i < lim
                tok = jnp.where(valid, tok_ref[0, 0, r0 + i], group)
                rows.append(pl.multiple_of(tok * slab, slab))
                wgts.append(jnp.where(valid, wgt_ref[0, 0, r0 + i], 0.0))
            cur = [y_ref[0, pl.ds(r, slab), :] for r in rows]
            new = [cur[k] + wgts[k] * obuf[pl.ds((b0 + k) * slab, slab), :] for k in range(MOE_SCATTER_BATCH)]
            for k in range(MOE_SCATTER_BATCH):
                y_ref[0, pl.ds(rows[k], slab), :] = new[k]

    @pl.when(n_chunks > 0)
    def _():
        gather(off)

    def one_pass(j, carry):
        r0 = off + j * chunk
        x = jnp.concatenate([xbuf[pl.ds(s, chunk, stride=slab), :] for s in range(slab)], axis=1)
        gather(r0 + chunk)
        scatter(r0 - chunk, j > 0)
        y = _expert_ffn(x.astype(BF16), wup_ref, bup_ref, wdn_ref, bdn_ref, ff_chunk)
        for s in range(slab):
            obuf[pl.ds(s, chunk, stride=slab), :] = y[:, s * LANES:(s + 1) * LANES]
        return carry

    lax.fori_loop(0, n_chunks, one_pass, 0)

    @pl.when(n_chunks > 0)
    def _():
        scatter(off + (n_chunks - 1) * chunk, True)


def _moe_grouped(h2_slabs, topw, topi, w_up_bf, b_up, w_dn_bf, b_dn, group):
    m = topw.shape[0]
    slab = h2_slabs.shape[0] // m
    n_exp, d, ff2 = w_up_bf.shape
    d_ff = ff2 // 2
    ngroups = m // group
    rows = group * TOP_K
    eid = topi[:, :TOP_K].reshape(ngroups, rows)
    order = jnp.argsort(eid, axis=1, stable=True)
    tok_sorted = (order // TOP_K).astype(jnp.int32)
    wgt_sorted = jnp.take_along_axis(topw[:, :TOP_K].reshape(ngroups, rows), order, axis=1)
    counts = jnp.sum((eid[:, :, None] == jnp.arange(n_exp, dtype=jnp.int32)).astype(jnp.int32), axis=1)
    offs = jnp.cumsum(counts, axis=1) - counts + MOE_CHUNK
    table = rows + 3 * MOE_CHUNK
    pad = ((0, 0), (MOE_CHUNK, 2 * MOE_CHUNK))
    tok_sorted = jnp.pad(tok_sorted, pad).reshape(ngroups, 1, table)
    wgt_sorted = jnp.pad(wgt_sorted, pad).reshape(ngroups, 1, table)

    grid_spec = pltpu.PrefetchScalarGridSpec(
        num_scalar_prefetch=2,
        grid=(ngroups, n_exp),
        in_specs=[
            pl.BlockSpec((1, 1, table), lambda g, e, c, o: (g, 0, 0), memory_space=pltpu.SMEM),
            pl.BlockSpec((1, 1, table), lambda g, e, c, o: (g, 0, 0), memory_space=pltpu.SMEM),
            pl.BlockSpec((1, group * slab, LANES), lambda g, e, c, o: (g, 0, 0), pipeline_mode=pl.Buffered(1)),
            pl.BlockSpec((1, d, ff2), lambda g, e, c, o: (e, 0, 0)),
            pl.BlockSpec((1, 1, ff2), lambda g, e, c, o: (e, 0, 0)),
            pl.BlockSpec((1, d_ff, d), lambda g, e, c, o: (e, 0, 0)),
            pl.BlockSpec((1, 1, d), lambda g, e, c, o: (e, 0, 0)),
        ],
        out_specs=pl.BlockSpec((1, (group + 1) * slab, LANES), lambda g, e, c, o: (g, 0, 0),
                               pipeline_mode=pl.Buffered(1)),
        scratch_shapes=[pltpu.VMEM((MOE_CHUNK * slab, LANES), F32), pltpu.VMEM((MOE_CHUNK * slab, LANES), F32)],
    )
    return pl.pallas_call(
        functools.partial(_moe_grouped_kernel, group=group, chunk=MOE_CHUNK, ff_chunk=min(512, d_ff)),
        grid_spec=grid_spec,
        out_shape=jax.ShapeDtypeStruct((ngroups, (group + 1) * slab, LANES), F32),
        compiler_params=pltpu.CompilerParams(dimension_semantics=("parallel", "arbitrary"),
                                             vmem_limit_bytes=MOE_VMEM_LIMIT),
        name="moe_grouped",
    )(counts.reshape(-1), offs.reshape(-1), tok_sorted, wgt_sorted,
      h2_slabs.reshape(ngroups, group * slab, LANES), w_up_bf, b_up.reshape(n_exp, 1, ff2), w_dn_bf,
      b_dn.reshape(n_exp, 1, d))


def _moe_finish_kernel(moe_ref, x1_ref, gtf_ref, nw_ref, y_ref):
    tt, d = x1_ref.shape[1], x1_ref.shape[2]
    slab = d // LANES
    moe = jnp.concatenate([moe_ref[0, pl.ds(s, tt, stride=slab), :] for s in range(slab)], axis=1)
    x = x1_ref[0] + gtf_ref[0] * moe
    y_ref[0] = x * lax.rsqrt(jnp.mean(x * x, axis=-1, keepdims=True) + EPS) * nw_ref[...]


def _moe_finish(moe_slabs, x1, gtf, norm_final_w, group, tt):
    n, t, d = x1.shape
    slab = d // LANES
    per_group = group // tt
    steps = t // tt
    return pl.pallas_call(
        _moe_finish_kernel,
        grid=(n, steps),
        in_specs=[
            pl.BlockSpec((1, tt * slab, LANES),
                         lambda i, j: ((i * steps + j) // per_group, (i * steps + j) % per_group, 0)),
            pl.BlockSpec((1, tt, d), lambda i, j: (i, j, 0)),
            pl.BlockSpec((1, 1, d), lambda i, j: (i, 0, 0)),
            pl.BlockSpec((1, d), lambda i, j: (0, 0)),
        ],
        out_specs=pl.BlockSpec((1, tt, d), lambda i, j: (i, j, 0)),
        out_shape=jax.ShapeDtypeStruct((n, t, d), F32),
        compiler_params=_cparams(("parallel", "parallel")),
        name="moe_finish",
    )(moe_slabs, x1, gtf, norm_final_w.reshape(1, d))


def _pick_tile(t, pref):
    tile = min(t, pref)
    assert t % tile == 0, (t, tile)
    return tile


def kernel(x_prompt, x_sample, cache_k, cache_v, state_conv, state_ssm, page_table, c_prompt, c_sample, w_ada, b_ada, norm_attn_w, norm_ffn_w, norm_final_w, w_in, rel_bias, conv_w, a_log, dt_bias, gdn_norm_w, w_out, w_router, b_router, w_up, b_up, w_down, b_down):
    depth = w_in.shape[0]
    assert depth == 1, "single-layer step"
    nb, seq, d = x_prompt.shape
    db, dec_seq, _ = x_sample.shape
    n_pool, page, kv_heads, dh = cache_k.shape[1:]
    assert dh == ATT_HEAD_DIM
    att_heads = rel_bias.shape[1]
    assert att_heads == kv_heads * ATT_GROUP
    gdn_heads = a_log.shape[1]
    n_pages = page_table.shape[1]
    past = n_pages * page
    q_w = att_heads * dh
    kv_w = kv_heads * dh
    gdn_cw = conv_w.shape[2]
    z_w = gdn_heads * GDN_DK
    assert gdn_cw == 3 * z_w
    in_w = q_w + 2 * kv_w + gdn_cw + z_w + 2 * gdn_heads
    assert w_in.shape[2] == in_w and in_w - 2 * gdn_heads == (in_w // LANES) * LANES
    assert seq % MOBA_BLOCK == 0 and past % MOBA_BLOCK == 0 and MOBA_BLOCK % page == 0
    assert dec_seq <= MOBA_BLOCK and dec_seq >= GDN_CONV - 1 and dec_seq <= SUBLANES
    n_exp = w_router.shape[2]
    nblk = seq // MOBA_BLOCK
    widths = (q_w, kv_w, gdn_cw, z_w)

    n_mod = nb + db
    n_mod_pad = -(-n_mod // SUBLANES) * SUBLANES
    c_all = jnp.pad(jnp.concatenate([c_prompt, c_sample], axis=0), ((0, n_mod_pad - n_mod), (0, 0)))
    mod = _ada(c_all, w_ada[0], b_ada[0])
    mod_p = [mod[:nb, i * d:(i + 1) * d].reshape(nb, 1, d) for i in range(6)]
    m_s = db * dec_seq
    mod_s = [jnp.broadcast_to(mod[nb:n_mod, None, i * d:(i + 1) * d], (db, dec_seq, d)).reshape(1, m_s, d)
             for i in range(6)]

    w_in_pad = jnp.pad(w_in[0], ((0, 0), (0, LANES - 2 * gdn_heads))).astype(BF16)
    w_out_bf = w_out[0].astype(BF16)
    wr_pad = jnp.pad(w_router[0], ((0, 0), (0, LANES - n_exp)))
    br_pad = jnp.pad(b_router[0], (0, LANES - n_exp), constant_values=NEG).reshape(1, LANES)
    w_up_bf = w_up[0].astype(BF16)
    w_dn_bf = w_down[0].astype(BF16)

    tt_p = _pick_tile(seq, 512)
    q_p, k_p, v_p, conv_p, z_p, ba_p, kh_p, vt_p, km_p = _inproj(
        x_prompt, mod_p[0], mod_p[1], norm_attn_w[0], w_in_pad, widths, tt_p, True)
    km_h = km_p.reshape(nb, nblk, kv_heads, dh).transpose(0, 2, 1, 3)
    bias_t = _bias_tiles(rel_bias, nblk)
    attn_p = _attn_prompt(q_p, kh_p, vt_p, km_h, bias_t)
    og_p, ssm_p = _gdn_prompt(conv_p, ba_p, conv_w[0], a_log[0], dt_bias[0], gdn_heads, _pick_tile(seq, 256))
    x1_p, h2_p, topw_p, topi_p = _mid(x_prompt, attn_p, og_p, z_p, mod_p[2], mod_p[4], mod_p[3], gdn_norm_w[0],
                                      w_out_bf, norm_ffn_w[0], wr_pad, br_pad, tt_p, gdn_heads, True)
    m_p = nb * seq
    group = _pick_tile(m_p, MOE_GROUP)
    assert group % tt_p == 0 and seq % tt_p == 0
    moe_p = _moe_grouped(h2_p.reshape(m_p * (d // LANES), LANES), topw_p.reshape(m_p, LANES),
                         topi_p.reshape(m_p, LANES), w_up_bf, b_up[0], w_dn_bf, b_down[0], group)
    y_p = _moe_finish(moe_p, x1_p, mod_p[5], norm_final_w, group, tt_p)

    tt_s = _pick_tile(m_s, 512)
    q_s, k_s, v_s, conv_s, z_s, ba_s = _inproj(
        x_sample.reshape(1, m_s, d), mod_s[0], mod_s[1], norm_attn_w[0], w_in_pad, widths, tt_s, False)
    q4 = q_s.reshape(db, dec_seq, kv_heads, ATT_GROUP, dh).transpose(0, 2, 3, 1, 4)
    q4 = q4.reshape(db, kv_heads, ATT_GROUP * dec_seq, 1, dh)
    eye = jnp.eye(kv_heads, dtype=F32).reshape(1, kv_heads, 1, kv_heads, 1)
    qc = (q4 * eye).reshape(db, att_heads * dec_seq, kv_w)
    new_pad = ((0, 0), (0, SUBLANES - dec_seq), (0, 0))
    k_new = jnp.pad(k_s.reshape(db, dec_seq, kv_w), new_pad)
    v_new = jnp.pad(v_s.reshape(db, dec_seq, kv_w), new_pad)
    bias_r = _bias_rows(rel_bias, past, dec_seq, past + MOBA_BLOCK)
    ck_t = cache_k[0].transpose(0, 2, 3, 1).reshape(n_pool, kv_w, page)
    cv_t = cache_v[0].transpose(0, 2, 3, 1).reshape(n_pool, kv_w, page)
    attn_rows = _attn_sample(page_table, qc, k_new, v_new, bias_r, ck_t, cv_t, dec_seq, kv_heads)
    attn_s = attn_rows.reshape(db, att_heads, dec_seq, dh).transpose(0, 2, 1, 3).reshape(1, m_s, q_w)

    conv_s3 = conv_s.reshape(db, dec_seq, gdn_cw)
    ext = jnp.concatenate([state_conv[0], conv_s3], axis=1)
    gs = _pick_tile(db, SUBLANES)
    og_tm, ssm_s = _gdn_sample(jnp.swapaxes(ext, 0, 1), jnp.swapaxes(ba_s.reshape(db, dec_seq, LANES), 0, 1),
                               conv_w[0], a_log[0], dt_bias[0], state_ssm[0], gdn_heads, gs)
    og_s = jnp.swapaxes(og_tm, 0, 1).reshape(1, m_s, z_w)
    x1_s, h2_s, gates_s = _mid(x_sample.reshape(1, m_s, d), attn_s, og_s, z_s, mod_s[2], mod_s[4], mod_s[3],
                               gdn_norm_w[0], w_out_bf, norm_ffn_w[0], wr_pad, br_pad, tt_s, gdn_heads, False)
    y_s = _moe(h2_s, gates_s, x1_s, mod_s[5], w_up_bf, b_up[0], w_dn_bf, b_down[0], norm_final_w, tt_s)

    return (
        y_p,
        y_s.reshape(db, dec_seq, d),
        k_p.reshape(1, nb, seq, kv_heads, dh),
        v_p.reshape(1, nb, seq, kv_heads, dh),
        conv_p[:, seq - (GDN_CONV - 1):, :][None],
        ssm_p[None],
        k_s.reshape(1, db, dec_seq, kv_heads, dh),
        v_s.reshape(1, db, dec_seq, kv_heads, dh),
        ext[:, dec_seq:, :][None],
        ssm_s[None],
    )
```

```python
import functools
import math

import jax
import jax.numpy as jnp
from jax import lax
from jax.experimental import pallas as pl
from jax.experimental.pallas import tpu as pltpu

F32 = jnp.float32
BF16 = jnp.bfloat16
HIGHEST = lax.Precision.HIGHEST

ATT_HEAD_DIM = 64
ATT_GROUP = 2
MOBA_BLOCK = 256
MOBA_TOPK = 3
REL_BUCKETS = 32
REL_MAX_DIST = 4096
GDN_DK = 128
GDN_CONV = 4
GDN_CHUNK = 64
TOP_K = 4
SWIGLU_LIMIT = 7.0
SWIGLU_ALPHA = 1.702
EPS = 1e-6
NEG = -1e30

LANES = 128
SUBLANES = 8
VT_ROWS = ATT_HEAD_DIM + SUBLANES
LOG2E = math.log2(math.e)
VMEM_LIMIT = 48 * 1024 * 1024
MOE_VMEM_LIMIT = 56 * 1024 * 1024


def _cparams(sem):
    return pltpu.CompilerParams(dimension_semantics=sem, vmem_limit_bytes=VMEM_LIMIT)


def _dot(a, b, precision=None):
    return jnp.dot(a, b, precision=precision, preferred_element_type=F32)


def _dot_nt(a, b, precision=None):
    return lax.dot_general(a, b, (((1,), (1,)), ((), ())), precision=precision,
                           preferred_element_type=F32)


def _dot_tn(a, b, precision=None):
    return lax.dot_general(a, b, (((0,), (0,)), ((), ())), precision=precision,
                           preferred_element_type=F32)


def _split(x):
    hi = x.astype(BF16)
    return hi, (x - hi.astype(F32)).astype(BF16)


def _mm3(a, b):
    m = a.shape[0]
    ah, al = _split(a)
    bh, bl = _split(b)
    t = _dot(jnp.concatenate([ah, al], axis=0), bh)
    return t[:m] + t[m:] + _dot(ah, bl)


def _mm3_nt(a, b):
    m = a.shape[0]
    ah, al = _split(a)
    bh, bl = _split(b)
    t = _dot_nt(jnp.concatenate([ah, al], axis=0), bh)
    return t[:m] + t[m:] + _dot_nt(ah, bl)


def _silu(x):
    return x * jax.nn.sigmoid(x)


def _softplus(x):
    return jnp.maximum(x, 0.0) + jnp.log1p(jnp.exp(-jnp.abs(x)))


def _rel_bucket(dist):
    n = jnp.maximum(dist, 0)
    max_exact = REL_BUCKETS // 2
    nf = jnp.maximum(n, max_exact).astype(F32)
    large = max_exact + (jnp.log(nf / max_exact) / math.log(REL_MAX_DIST / max_exact)
                         * (REL_BUCKETS - max_exact)).astype(jnp.int32)
    return jnp.where(n < max_exact, n, jnp.minimum(large, REL_BUCKETS - 1))


def _ada_kernel(c_ref, w_ref, b_ref, o_ref):
    o_ref[...] = _dot(_silu(c_ref[...]), w_ref[...], HIGHEST) + b_ref[...]


def _ada(c, w_ada, b_ada):
    m, d = c.shape
    n = w_ada.shape[1]
    tn = d
    return pl.pallas_call(
        _ada_kernel,
        grid=(n // tn,),
        in_specs=[pl.BlockSpec((m, d), lambda j: (0, 0)),
                  pl.BlockSpec((d, tn), lambda j: (0, j)),
                  pl.BlockSpec((1, tn), lambda j: (0, j))],
        out_specs=pl.BlockSpec((m, tn), lambda j: (0, j)),
        out_shape=jax.ShapeDtypeStruct((m, n), F32),
        compiler_params=_cparams(("parallel",)),
        name="ada_modulation",
    )(c, w_ada, b_ada.reshape(1, n))


def _inproj_kernel(x_ref, sh_ref, sc_ref, nw_ref, w_ref, *out_refs, offs, heads_out, kv_heads):
    x = x_ref[0]
    xn = x * lax.rsqrt(jnp.mean(x * x, axis=-1, keepdims=True) + EPS) * nw_ref[...]
    h = (xn * (1.0 + sc_ref[0]) + sh_ref[0]).astype(BF16)
    o0, o1, o2, o3, o4, o5 = offs

    def proj(lo, hi):
        return _dot(h, w_ref[:, lo:hi])

    q_ref, k_ref, v_ref, conv_ref, z_ref, ba_ref = out_refs[:6]
    q_ref[0] = proj(o0, o1)
    k = proj(o1, o2)
    v = proj(o2, o3)
    k_ref[0] = k
    v_ref[0] = v
    conv_ref[0] = proj(o3, o4)
    z_ref[0] = proj(o4, o5)
    ba_ref[0] = proj(o5, o5 + LANES)
    if heads_out:
        kh_ref, vt_ref, km_ref = out_refs[6:]
        vt = v.T
        for b in range(x.shape[0] // MOBA_BLOCK):
            rows = slice(b * MOBA_BLOCK, (b + 1) * MOBA_BLOCK)
            kb = k[rows]
            km_ref[0, b] = jnp.mean(kb, axis=0, keepdims=True)
            for hh in range(kv_heads):
                cols = slice(hh * ATT_HEAD_DIM, (hh + 1) * ATT_HEAD_DIM)
                kh_ref[0, hh, b] = kb[:, cols].astype(BF16)
                vt_ref[0, hh, b, 0:ATT_HEAD_DIM, :] = vt[cols, rows].astype(BF16)
                vt_ref[0, hh, b, ATT_HEAD_DIM:VT_ROWS, :] = jnp.ones((VT_ROWS - ATT_HEAD_DIM, MOBA_BLOCK), BF16)


def _inproj(x, sh, sc, norm_w, w_pad, widths, tt, heads_out):
    n, t, d = x.shape
    per_token = sh.shape[1] != 1
    q_w, kv_w, conv_w_, z_w = widths
    offs = (0, q_w, q_w + kv_w, q_w + 2 * kv_w, q_w + 2 * kv_w + conv_w_, q_w + 2 * kv_w + conv_w_ + z_w)
    kv_heads = kv_w // ATT_HEAD_DIM
    nb_t = tt // MOBA_BLOCK
    mod_spec = (pl.BlockSpec((1, tt, d), lambda i, j: (i, j, 0)) if per_token
                else pl.BlockSpec((1, 1, d), lambda i, j: (i, 0, 0)))

    def tok(width):
        return pl.BlockSpec((1, tt, width), lambda i, j: (i, j, 0))

    out_specs = [tok(q_w), tok(kv_w), tok(kv_w), tok(conv_w_), tok(z_w), tok(LANES)]
    out_shape = [jax.ShapeDtypeStruct((n, t, w), F32) for w in (q_w, kv_w, kv_w, conv_w_, z_w, LANES)]
    if heads_out:
        nblk = t // MOBA_BLOCK
        out_specs += [
            pl.BlockSpec((1, kv_heads, nb_t, MOBA_BLOCK, ATT_HEAD_DIM), lambda i, j: (i, 0, j, 0, 0)),
            pl.BlockSpec((1, kv_heads, nb_t, VT_ROWS, MOBA_BLOCK), lambda i, j: (i, 0, j, 0, 0)),
            pl.BlockSpec((1, nb_t, 1, kv_w), lambda i, j: (i, j, 0, 0)),
        ]
        out_shape += [
            jax.ShapeDtypeStruct((n, kv_heads, nblk, MOBA_BLOCK, ATT_HEAD_DIM), BF16),
            jax.ShapeDtypeStruct((n, kv_heads, nblk, VT_ROWS, MOBA_BLOCK), BF16),
            jax.ShapeDtypeStruct((n, nblk, 1, kv_w), F32),
        ]
    return pl.pallas_call(
        functools.partial(_inproj_kernel, offs=offs, heads_out=heads_out, kv_heads=kv_heads),
        grid=(n, t // tt),
        in_specs=[pl.BlockSpec((1, tt, d), lambda i, j: (i, j, 0)), mod_spec, mod_spec,
                  pl.BlockSpec((1, d), lambda i, j: (0, 0)),
                  pl.BlockSpec(w_pad.shape, lambda i, j: (0, 0))],
        out_specs=out_specs,
        out_shape=out_shape,
        compiler_params=_cparams(("parallel", "parallel")),
        name="in_proj_prompt" if heads_out else "in_proj_sample",
    )(x, sh, sc, norm_w.reshape(1, d), w_pad)


def _bias_tile_kernel(rb_ref, o_ref):
    kvh = pl.program_id(0)
    delta = pl.num_programs(1) - 1 - pl.program_id(1)
    key = lax.broadcasted_iota(jnp.int32, (MOBA_BLOCK, MOBA_BLOCK), 0)
    qry = lax.broadcasted_iota(jnp.int32, (MOBA_BLOCK, MOBA_BLOCK), 1)
    bucket = _rel_bucket(delta * MOBA_BLOCK + qry - key)
    for g in range(ATT_GROUP):
        acc = jnp.zeros((MOBA_BLOCK, MOBA_BLOCK), F32)
        for t in range(REL_BUCKETS):
            acc = jnp.where(bucket == t, rb_ref[t, kvh * ATT_GROUP + g], acc)
        o_ref[0, 0, :, g * MOBA_BLOCK:(g + 1) * MOBA_BLOCK] = acc * LOG2E


def _bias_tiles(rel_bias, nblk):
    kv_heads = rel_bias.shape[1] // ATT_GROUP
    width = ATT_GROUP * MOBA_BLOCK
    return pl.pallas_call(
        _bias_tile_kernel,
        grid=(kv_heads, nblk),
        in_specs=[pl.BlockSpec(memory_space=pltpu.SMEM)],
        out_specs=pl.BlockSpec((1, 1, MOBA_BLOCK, width), lambda h, dlt: (h, dlt, 0, 0)),
        out_shape=jax.ShapeDtypeStruct((kv_heads, nblk, MOBA_BLOCK, width), F32),
        compiler_params=_cparams(("parallel", "parallel")),
        name="rel_bias_tiles",
    )(rel_bias)


def _bias_rows_kernel(tab_ref, o_ref, *, past, dec_seq):
    rows, width = o_ref.shape
    row = lax.broadcasted_iota(jnp.int32, (rows, width), 0)
    key = lax.broadcasted_iota(jnp.int32, (rows, width), 1)
    bucket = _rel_bucket(past + lax.rem(row, dec_seq) - key)
    acc = jnp.zeros((rows, width), F32)
    for t in range(REL_BUCKETS):
        acc = jnp.where(bucket == t, tab_ref[:, t:t + 1], acc)
    o_ref[...] = acc


def _bias_rows(rel_bias, past, dec_seq, width):
    heads = rel_bias.shape[1]
    rows = heads * dec_seq
    tab = jnp.repeat(rel_bias.T, dec_seq, axis=0)
    return pl.pallas_call(
        functools.partial(_bias_rows_kernel, past=past, dec_seq=dec_seq),
        out_shape=jax.ShapeDtypeStruct((rows, width), F32),
        name="rel_bias_rows",
    )(tab)


def _topk_rank_rows(gm, idx, n):
    rank = jnp.zeros(gm.shape, jnp.int32)
    for m in range(n):
        row = gm[m:m + 1, :]
        beats = (row > gm) | ((row == gm) & (m < idx))
        rank = rank + beats.astype(jnp.int32)
    return rank


def _attn_prompt_kernel(q_ref, kh_ref, vt_ref, km_ref, bias_ref, o_ref, sel_ref, raw_ref, *, nblk, hps):
    qt = pl.program_id(2)
    q_t = q_ref[0].T
    tq = q_t.shape[1]
    width = ATT_GROUP * tq
    heads = range(hps)
    blk = lax.broadcasted_iota(jnp.int32, (nblk, width), 0)
    past = blk < qt
    key = lax.broadcasted_iota(jnp.int32, (MOBA_BLOCK, width), 0)
    qry = lax.rem(lax.broadcasted_iota(jnp.int32, (MOBA_BLOCK, width), 1), tq)
    own_tile = nblk - 1
    odd = lax.rem(qt, 2)
    n_pairs = qt // 2

    qs, carry = [], []
    for hd in heads:
        q_all = jnp.concatenate([q_t[(hd * ATT_GROUP + g) * ATT_HEAD_DIM:(hd * ATT_GROUP + g + 1) * ATT_HEAD_DIM, :]
                                 for g in range(ATT_GROUP)], axis=1)
        gate = _dot(km_ref[0, hd], q_all, HIGHEST)
        rank = _topk_rank_rows(jnp.where(past, gate, NEG), blk, nblk)
        sel_ref[hd] = (past & (rank < MOBA_TOPK)).astype(F32)
        qs.append((q_all * (ATT_HEAD_DIM ** -0.5 * LOG2E)).astype(BF16))
        s = jnp.where(key <= qry, _dot(kh_ref[0, hd, qt], qs[hd]) + bias_ref[hd, own_tile], NEG)
        m0 = jnp.max(s, axis=0, keepdims=True)
        carry += [m0, _dot(vt_ref[0, hd, qt], jnp.exp2(s - m0).astype(BF16))]

    def update(m, acc, s, vt):
        m_new = jnp.maximum(m, jnp.max(s, axis=0, keepdims=True))
        return [m_new, jnp.exp2(m - m_new) * acc + _dot(vt, jnp.exp2(s - m_new).astype(BF16))]

    def single(carry):
        out = []
        for hd in heads:
            s = _dot(kh_ref[0, hd, 0], qs[hd]) + bias_ref[hd, own_tile - qt]
            out += update(carry[2 * hd], carry[2 * hd + 1], jnp.where(sel_ref[hd, 0:1, :] > 0.5, s, NEG),
                          vt_ref[0, hd, 0])
        return tuple(out)

    carry = lax.cond(odd == 1, single, lambda c: c, tuple(carry))

    def raw_scores(hd, i):
        kb = jnp.minimum(odd + 2 * jnp.minimum(i, n_pairs - 1), nblk - 2)
        kb = jnp.maximum(kb, 0)
        return _dot(kh_ref[0, hd, pl.ds(kb, 2)].reshape(2 * MOBA_BLOCK, ATT_HEAD_DIM), qs[hd])

    for hd in heads:
        raw_ref[hd] = raw_scores(hd, 0)

    def pair(i, carry):
        kb = odd + 2 * i
        out = []
        for hd in heads:
            s = raw_ref[hd] + bias_ref[hd, pl.ds(own_tile - qt + kb, 2)].reshape(2 * MOBA_BLOCK, width)
            raw_ref[hd] = raw_scores(hd, i + 1)
            s = jnp.concatenate([jnp.where(sel_ref[hd, pl.ds(kb, 1), :] > 0.5, s[:MOBA_BLOCK], NEG),
                                 jnp.where(sel_ref[hd, pl.ds(kb + 1, 1), :] > 0.5, s[MOBA_BLOCK:], NEG)], axis=0)
            vt2 = jnp.concatenate([vt_ref[0, hd, kb], vt_ref[0, hd, kb + 1]], axis=1)
            out += update(carry[2 * hd], carry[2 * hd + 1], s, vt2)
        return tuple(out)

    carry = lax.fori_loop(0, n_pairs, pair, carry)
    outs = []
    for hd in heads:
        acc = carry[2 * hd + 1]
        out_t = acc[:ATT_HEAD_DIM] / acc[ATT_HEAD_DIM:ATT_HEAD_DIM + 1]
        outs += [out_t[:, g * tq:(g + 1) * tq].T for g in range(ATT_GROUP)]
    o_ref[0] = jnp.concatenate(outs, axis=1)


ATTN_HEADS_PER_STEP = 2


def _attn_prompt(q, kh, vt, km, bias):
    n, t, q_w = q.shape
    kv_heads, nblk = kh.shape[1], kh.shape[2]
    assert nblk >= 2, "the look-ahead in the block-pair loop reads two key blocks"
    hps = ATTN_HEADS_PER_STEP
    assert kv_heads % hps == 0
    gw = hps * ATT_GROUP * ATT_HEAD_DIM
    width = ATT_GROUP * MOBA_BLOCK
    return pl.pallas_call(
        functools.partial(_attn_prompt_kernel, nblk=nblk, hps=hps),
        grid=(kv_heads // hps, n, nblk),
        in_specs=[
            pl.BlockSpec((1, MOBA_BLOCK, gw), lambda h, i, j: (i, j, h)),
            pl.BlockSpec((1, hps, nblk, MOBA_BLOCK, ATT_HEAD_DIM), lambda h, i, j: (i, h, 0, 0, 0)),
            pl.BlockSpec((1, hps, nblk, VT_ROWS, MOBA_BLOCK), lambda h, i, j: (i, h, 0, 0, 0)),
            pl.BlockSpec((1, hps, nblk, ATT_HEAD_DIM), lambda h, i, j: (i, h, 0, 0)),
            pl.BlockSpec((hps, nblk, MOBA_BLOCK, width), lambda h, i, j: (h, 0, 0, 0), pipeline_mode=pl.Buffered(1)),
        ],
        out_specs=pl.BlockSpec((1, MOBA_BLOCK, gw), lambda h, i, j: (i, j, h)),
        out_shape=jax.ShapeDtypeStruct((n, t, q_w), F32),
        scratch_shapes=[pltpu.VMEM((hps, nblk, width), F32),
                        pltpu.VMEM((hps, 2 * MOBA_BLOCK, width), F32)],
        compiler_params=_cparams(("parallel", "parallel", "arbitrary")),
        name="moba_prompt",
    )(q, kh, vt, km, bias)


def _attn_sample_kernel(pt_ref, qc_ref, knew_ref, vnew_ref, bias_ref, ck_hbm, cv_hbm, o_ref,
                        kbuf, vbuf, sem, *, n_pages, dec_seq, kv_heads):
    s = pl.program_id(0)
    nseq = pl.num_programs(0)
    slot = lax.rem(s, 2)
    page = kbuf.shape[3]
    pages_per_blk = MOBA_BLOCK // page
    nblk = n_pages // pages_per_blk
    past_len = nblk * MOBA_BLOCK

    def start_fetch(seq, sl):
        def one(p, carry):
            pg = pt_ref[seq, p]
            pltpu.make_async_copy(ck_hbm.at[pg], kbuf.at[sl, p], sem.at[0, sl]).start()
            pltpu.make_async_copy(cv_hbm.at[pg], vbuf.at[sl, p], sem.at[1, sl]).start()
            return carry
        lax.fori_loop(0, n_pages, one, 0)

    @pl.when(s == 0)
    def _():
        start_fetch(0, 0)

    @pl.when(s + 1 < nseq)
    def _():
        start_fetch(s + 1, 1 - slot)

    pltpu.make_async_copy(ck_hbm.at[pl.ds(0, n_pages)], kbuf.at[slot], sem.at[0, slot]).wait()
    pltpu.make_async_copy(cv_hbm.at[pl.ds(0, n_pages)], vbuf.at[slot], sem.at[1, slot]).wait()

    qc = qc_ref[0]
    rows, kv_w = qc.shape
    scale = ATT_HEAD_DIM ** -0.5
    q_hi, q_lo = _split(qc)
    q_both = jnp.concatenate([q_hi, q_lo], axis=0)

    s_pages = []
    for p in range(n_pages):
        k_hi, k_lo = _split(kbuf[slot, p])
        t = _dot(q_both, k_hi)
        s_pages.append(t[:rows] + t[rows:] + _dot(q_hi, k_lo))

    lane = lax.broadcasted_iota(jnp.int32, (rows, LANES), 1)
    gate = jnp.zeros((rows, LANES), F32)
    for b in range(nblk):
        tot = s_pages[b * pages_per_blk]
        for p in range(b * pages_per_blk + 1, (b + 1) * pages_per_blk):
            tot = tot + s_pages[p]
        gate = jnp.where(lane == b, jnp.sum(tot, axis=1, keepdims=True) * (1.0 / MOBA_BLOCK), gate)
    rank = jnp.zeros((rows, LANES), jnp.int32)
    for m in range(nblk):
        col = gate[:, m:m + 1]
        rank = rank + ((col > gate) | ((col == gate) & (m < lane))).astype(jnp.int32)
    sel = rank < MOBA_TOPK

    masked = []
    for p in range(n_pages):
        b = p // pages_per_blk
        sp = s_pages[p] * scale + bias_ref[:, p * page:(p + 1) * page]
        masked.append(jnp.where(sel[:, b:b + 1], sp, NEG))
    n_new = knew_ref.shape[1]
    r_idx = lax.rem(lax.broadcasted_iota(jnp.int32, (rows, n_new), 0), dec_seq)
    j_idx = lax.broadcasted_iota(jnp.int32, (rows, n_new), 1)
    s_own = _dot_nt(q_hi, knew_ref[0].astype(BF16)) * scale + bias_ref[:, past_len:past_len + n_new]
    s_own = jnp.where(j_idx <= r_idx, s_own, NEG)
    m = jnp.max(s_own, axis=1, keepdims=True)
    for sp in masked:
        m = jnp.maximum(m, jnp.max(sp, axis=1, keepdims=True))
    p_own = jnp.exp(s_own - m)
    l = jnp.sum(p_own, axis=1, keepdims=True)
    acc = _dot(p_own.astype(BF16), vnew_ref[0].astype(BF16))
    for p, sp in enumerate(masked):
        prob = jnp.exp(sp - m)
        l = l + jnp.sum(prob, axis=1, keepdims=True)
        acc = acc + _dot_nt(prob.astype(BF16), vbuf[slot, p].astype(BF16))
    out = acc / l
    rpk = rows // kv_heads
    for h in range(kv_heads):
        o_ref[0, h * rpk:(h + 1) * rpk, :] = out[h * rpk:(h + 1) * rpk,
                                                 h * ATT_HEAD_DIM:(h + 1) * ATT_HEAD_DIM]


def _attn_sample(page_table, qc, k_new, v_new, bias, cache_k, cache_v, dec_seq, kv_heads):
    b, rows, kv_w = qc.shape
    n_pages = page_table.shape[1]
    page = cache_k.shape[2]
    n_new = k_new.shape[1]
    grid_spec = pltpu.PrefetchScalarGridSpec(
        num_scalar_prefetch=1,
        grid=(b,),
        in_specs=[
            pl.BlockSpec((1, rows, kv_w), lambda i, pt: (i, 0, 0)),
            pl.BlockSpec((1, n_new, kv_w), lambda i, pt: (i, 0, 0)),
            pl.BlockSpec((1, n_new, kv_w), lambda i, pt: (i, 0, 0)),
            pl.BlockSpec(bias.shape, lambda i, pt: (0, 0)),
            pl.BlockSpec(memory_space=pl.ANY),
            pl.BlockSpec(memory_space=pl.ANY),
        ],
        out_specs=pl.BlockSpec((1, rows, ATT_HEAD_DIM), lambda i, pt: (i, 0, 0)),
        scratch_shapes=[pltpu.VMEM((2, n_pages, kv_w, page), F32),
                        pltpu.VMEM((2, n_pages, kv_w, page), F32),
                        pltpu.SemaphoreType.DMA((2, 2))],
    )
    return pl.pallas_call(
        functools.partial(_attn_sample_kernel, n_pages=n_pages, dec_seq=dec_seq, kv_heads=kv_heads),
        grid_spec=grid_spec,
        out_shape=jax.ShapeDtypeStruct((b, rows, ATT_HEAD_DIM), F32),
        compiler_params=_cparams(("arbitrary",)),
        name="moba_sample",
    )(page_table, qc, k_new, v_new, bias, cache_k, cache_v)


def _gdn_gates(ba, alog, dt):
    return jax.nn.sigmoid(ba), -jnp.exp(alog) * _softplus(ba + dt)


def _gdn_prompt_kernel(x_ref, ba_ref, bat_ref, cw_ref, alog_ref, dt_ref, alogt_ref, dtt_ref,
                       o_ref, s_ref, ext_ref, *, heads):
    ct, cw = x_ref.shape[1], x_ref.shape[2]
    qk_w = heads * GDN_DK
    cs = GDN_CHUNK
    halo = SUBLANES

    @pl.when(pl.program_id(1) == 0)
    def _():
        ext_ref[0:halo, :] = jnp.zeros((halo, cw), F32)
        s_ref[...] = jnp.zeros(s_ref.shape, F32)

    x = x_ref[0]
    ext_ref[halo:halo + ct, :] = x
    w = cw_ref[...]
    conv = x * w[GDN_CONV - 1:GDN_CONV, :]
    for i in range(GDN_CONV - 1):
        conv = conv + ext_ref[pl.ds(halo - (GDN_CONV - 1) + i, ct), :] * w[i:i + 1, :]
    ext_ref[0:halo, :] = x[ct - halo:ct, :]
    act = _silu(conv)

    beta_f, g_f = _gdn_gates(ba_ref[0], alog_ref[...], dt_ref[...])
    _, g_t = _gdn_gates(bat_ref[0], alogt_ref[...], dtt_ref[...])

    r_i = lax.broadcasted_iota(jnp.int32, (cs, cs), 0)
    c_i = lax.broadcasted_iota(jnp.int32, (cs, cs), 1)
    tril = r_i >= c_i
    strict = r_i > c_i
    eye = (r_i == c_i).astype(F32)
    cum_l = tril.astype(F32)
    cum_u = (r_i <= c_i).astype(F32)

    qn, kn, vv = [], [], []
    for h in range(heads):
        qh = act[:, h * GDN_DK:(h + 1) * GDN_DK]
        kh = act[:, qk_w + h * GDN_DK:qk_w + (h + 1) * GDN_DK]
        qn.append(qh * lax.rsqrt(jnp.sum(qh * qh, axis=-1, keepdims=True) + EPS) * (GDN_DK ** -0.5))
        kn.append(kh * lax.rsqrt(jnp.sum(kh * kh, axis=-1, keepdims=True) + EPS))
        vv.append(act[:, 2 * qk_w + h * GDN_DK:2 * qk_w + (h + 1) * GDN_DK])

    n_chunks = ct // cs
    units = [(c, h) for c in range(n_chunks) for h in range(heads)]
    gc, gct = [], []
    for c in range(n_chunks):
        rows = slice(c * cs, (c + 1) * cs)
        gc.append(_dot(cum_l, g_f[rows, :], HIGHEST))
        gct.append(_dot(g_t[:, rows], cum_u, HIGHEST))
    q_u, k_u, kbeta_u, vbeta_u, gcc_u, egc_u, decay_u = {}, {}, {}, {}, {}, {}, {}
    for (c, h) in units:
        rows = slice(c * cs, (c + 1) * cs)
        beta = beta_f[rows, h:h + 1]
        gcc = gc[c][:, heads + h:heads + h + 1]
        gcr = gct[c][heads + h:heads + h + 1, :]
        q_u[c, h], k_u[c, h] = qn[h][rows], kn[h][rows]
        kbeta_u[c, h] = k_u[c, h] * beta
        vbeta_u[c, h] = vv[h][rows] * beta
        gcc_u[c, h], egc_u[c, h] = gcc, jnp.exp(gcc)
        decay_u[c, h] = jnp.where(tril, jnp.exp(jnp.where(tril, gcc - gcr, 0.0)), 0.0)
    a_u = {u: _mm3_nt(jnp.concatenate([kbeta_u[u], q_u[u]], axis=0), k_u[u]) for u in units}
    x_u = {u: -jnp.where(strict, a_u[u][:cs] * decay_u[u], 0.0) for u in units}
    intra_u = {u: jnp.where(tril, a_u[u][cs:] * decay_u[u], 0.0) for u in units}
    p_u = {u: eye + x_u[u] for u in units}
    xp_u = {u: _mm3(x_u[u], x_u[u]) for u in units}
    npow = 2
    while npow * 2 < cs:
        pr_u = {u: _mm3(jnp.concatenate([p_u[u], xp_u[u]], axis=0), xp_u[u]) for u in units}
        p_u = {u: p_u[u] + pr_u[u][:cs] for u in units}
        xp_u = {u: pr_u[u][cs:] for u in units}
        npow *= 2
    p_u = {u: p_u[u] + _mm3(p_u[u], xp_u[u]) for u in units}
    uw_u = {u: _dot(p_u[u].astype(BF16),
                    jnp.concatenate([vbeta_u[u], kbeta_u[u] * egc_u[u]], axis=1).astype(BF16)) for u in units}

    state = [s_ref[0, h] for h in range(heads)]
    for c in range(n_chunks):
        rows = slice(c * cs, (c + 1) * cs)
        ws = [_dot(jnp.concatenate([uw_u[c, h][:, GDN_DK:], q_u[c, h] * egc_u[c, h]], axis=0).astype(BF16),
                   state[h].astype(BF16)) for h in range(heads)]
        v_new = [uw_u[c, h][:, :GDN_DK] - ws[h][:cs] for h in range(heads)]
        v_bf = [v.astype(BF16) for v in v_new]
        for h in range(heads):
            o_ref[0, rows, h * GDN_DK:(h + 1) * GDN_DK] = ws[h][cs:] + _dot(intra_u[c, h].astype(BF16), v_bf[h])
        for h in range(heads):
            gcc = gcc_u[c, h]
            g_last = gcc[cs - 1:cs, :]
            state[h] = state[h] * jnp.exp(g_last) + _dot_tn(
                (k_u[c, h] * jnp.exp(g_last - gcc)).astype(BF16), v_bf[h])
    for h in range(heads):
        s_ref[0, h] = state[h]


def _gdn_prompt(conv_pre, ba, conv_w, a_log, dt_bias, heads, ct):
    n, t, cw = conv_pre.shape
    bat = jnp.swapaxes(ba[:, :, :2 * heads], 1, 2)
    pad = jnp.zeros((heads,), F32)
    alog = jnp.concatenate([pad, a_log])
    dt = jnp.concatenate([pad, dt_bias])
    lane_pad = (0, LANES - 2 * heads)
    return pl.pallas_call(
        functools.partial(_gdn_prompt_kernel, heads=heads),
        grid=(n, t // ct),
        in_specs=[
            pl.BlockSpec((1, ct, cw), lambda i, j: (i, j, 0)),
            pl.BlockSpec((1, ct, LANES), lambda i, j: (i, j, 0)),
            pl.BlockSpec((1, 2 * heads, ct), lambda i, j: (i, 0, j)),
            pl.BlockSpec(conv_w.shape, lambda i, j: (0, 0)),
            pl.BlockSpec((1, LANES), lambda i, j: (0, 0)),
            pl.BlockSpec((1, LANES), lambda i, j: (0, 0)),
            pl.BlockSpec((2 * heads, 1), lambda i, j: (0, 0)),
            pl.BlockSpec((2 * heads, 1), lambda i, j: (0, 0)),
        ],
        out_specs=[pl.BlockSpec((1, ct, heads * GDN_DK), lambda i, j: (i, j, 0)),
                   pl.BlockSpec((1, heads, GDN_DK, GDN_DK), lambda i, j: (i, 0, 0, 0))],
        out_shape=[jax.ShapeDtypeStruct((n, t, heads * GDN_DK), F32),
                   jax.ShapeDtypeStruct((n, heads, GDN_DK, GDN_DK), F32)],
        scratch_shapes=[pltpu.VMEM((SUBLANES + ct, cw), F32)],
        compiler_params=_cparams(("parallel", "arbitrary")),
        name="gdn_prompt",
    )(conv_pre, ba, bat, conv_w, jnp.pad(alog, lane_pad).reshape(1, LANES),
      jnp.pad(dt, lane_pad).reshape(1, LANES), alog.reshape(2 * heads, 1), dt.reshape(2 * heads, 1))


def _gdn_sample_kernel(ext_ref, ba_ref, cw_ref, alog_ref, dt_ref, s_in_ref, o_ref, s_out_ref, *, heads):
    t_len = ba_ref.shape[0]
    gs = ext_ref.shape[1]
    qk_w = heads * GDN_DK
    w = cw_ref[...]
    q_t, k_t, v_r, beta_r, dec_r = [], [], [], [], []
    for t in range(t_len):
        conv = ext_ref[t] * w[0:1, :]
        for i in range(1, GDN_CONV):
            conv = conv + ext_ref[t + i] * w[i:i + 1, :]
        act = _silu(conv)
        beta_f, g_f = _gdn_gates(ba_ref[t], alog_ref[...], dt_ref[...])
        beta_r.append(beta_f)
        dec_r.append(jnp.exp(g_f))
        qs, ks, vs = [], [], []
        for h in range(heads):
            qh = act[:, h * GDN_DK:(h + 1) * GDN_DK]
            kh = act[:, qk_w + h * GDN_DK:qk_w + (h + 1) * GDN_DK]
            qn = qh * lax.rsqrt(jnp.sum(qh * qh, axis=-1, keepdims=True) + EPS) * (GDN_DK ** -0.5)
            kn = kh * lax.rsqrt(jnp.sum(kh * kh, axis=-1, keepdims=True) + EPS)
            qs.append(qn.T)
            ks.append(kn.T)
            vs.append(act[:, 2 * qk_w + h * GDN_DK:2 * qk_w + (h + 1) * GDN_DK])
        q_t.append(qs)
        k_t.append(ks)
        v_r.append(vs)
    for i in range(gs):
        for h in range(heads):
            st = s_in_ref[i, h]
            for t in range(t_len):
                st = st * dec_r[t][i:i + 1, heads + h:heads + h + 1]
                kcol = k_t[t][h][:, i:i + 1]
                ks_row = jnp.sum(st * kcol, axis=0, keepdims=True)
                delta = (v_r[t][h][i:i + 1, :] - ks_row) * beta_r[t][i:i + 1, h:h + 1]
                st = st + kcol * delta
                o_ref[t, i:i + 1, h * GDN_DK:(h + 1) * GDN_DK] = jnp.sum(
                    st * q_t[t][h][:, i:i + 1], axis=0, keepdims=True)
            s_out_ref[i, h] = st


def _gdn_sample(ext_tm, ba_tm, conv_w, a_log, dt_bias, state, heads, gs):
    t_len, b, _ = ba_tm.shape
    cw = ext_tm.shape[2]
    pad = jnp.zeros((heads,), F32)
    lane_pad = (0, LANES - 2 * heads)
    alog = jnp.pad(jnp.concatenate([pad, a_log]), lane_pad).reshape(1, LANES)
    dt = jnp.pad(jnp.concatenate([pad, dt_bias]), lane_pad).reshape(1, LANES)
    return pl.pallas_call(
        functools.partial(_gdn_sample_kernel, heads=heads),
        grid=(b // gs,),
        in_specs=[
            pl.BlockSpec((ext_tm.shape[0], gs, cw), lambda i: (0, i, 0)),
            pl.BlockSpec((t_len, gs, LANES), lambda i: (0, i, 0)),
            pl.BlockSpec(conv_w.shape, lambda i: (0, 0)),
            pl.BlockSpec((1, LANES), lambda i: (0, 0)),
            pl.BlockSpec((1, LANES), lambda i: (0, 0)),
            pl.BlockSpec((gs, heads, GDN_DK, GDN_DK), lambda i: (i, 0, 0, 0)),
        ],
        out_specs=[pl.BlockSpec((t_len, gs, heads * GDN_DK), lambda i: (0, i, 0)),
                   pl.BlockSpec((gs, heads, GDN_DK, GDN_DK), lambda i: (i, 0, 0, 0))],
        out_shape=[jax.ShapeDtypeStruct((t_len, b, heads * GDN_DK), F32),
                   jax.ShapeDtypeStruct(state.shape, F32)],
        compiler_params=_cparams(("parallel",)),
        name="gdn_sample",
    )(ext_tm, ba_tm, conv_w, alog, dt, state)


def _mid_kernel(x_ref, attn_ref, og_ref, z_ref, gta_ref, scf_ref, shf_ref, gw_ref, wout_ref, nf_ref,
                wr_ref, br_ref, x1_ref, h2_ref, gates_ref, topi_ref=None, *, heads, grouped):
    og = og_ref[0]
    z = z_ref[0]
    parts = [attn_ref[0]]
    for h in range(heads):
        o = og[:, h * GDN_DK:(h + 1) * GDN_DK]
        y = o * lax.rsqrt(jnp.mean(o * o, axis=-1, keepdims=True) + EPS) * gw_ref[...]
        parts.append(y * _silu(z[:, h * GDN_DK:(h + 1) * GDN_DK]))
    mix = jnp.concatenate(parts, axis=1).astype(BF16)
    x1 = x_ref[0] + gta_ref[0] * _dot(mix, wout_ref[...])
    x1_ref[0] = x1
    h2 = (x1 * lax.rsqrt(jnp.mean(x1 * x1, axis=-1, keepdims=True) + EPS) * nf_ref[...]
          * (1.0 + scf_ref[0]) + shf_ref[0])
    logits = _dot(h2, wr_ref[...], HIGHEST) + br_ref[...]
    lane = lax.broadcasted_iota(jnp.int32, logits.shape, 1)
    work = logits
    sel = jnp.zeros(logits.shape, jnp.bool_)
    vals, idxs = [], []
    for j in range(TOP_K):
        mx = jnp.max(work, axis=-1, keepdims=True)
        idx = jnp.min(jnp.where(work == mx, lane, LANES), axis=-1, keepdims=True)
        pick = lane == idx
        vals.append(mx)
        idxs.append(idx)
        sel = sel | pick
        work = jnp.where(pick, -jnp.inf, work)
    if grouped:
        tt = h2.shape[0]
        for s in range(h2.shape[1] // LANES):
            h2_ref[0, pl.ds(s, tt, stride=h2.shape[1] // LANES), :] = h2[:, s * LANES:(s + 1) * LANES]
        ex = [jnp.exp(v - vals[0]) for v in vals]
        denom = ex[0]
        for v in ex[1:]:
            denom = denom + v
        topi = jnp.zeros(logits.shape, jnp.int32)
        topw = jnp.zeros(logits.shape, F32)
        for j in range(TOP_K):
            topi = jnp.where(lane == j, idxs[j], topi)
            topw = jnp.where(lane == j, ex[j] / denom, topw)
        gates_ref[0] = topw
        topi_ref[0] = topi
    else:
        h2_ref[0] = h2.astype(BF16)
        e = jnp.where(sel, jnp.exp(logits - vals[0]), 0.0)
        gates_ref[0] = e / jnp.sum(e, axis=-1, keepdims=True)


def _mid(x, attn, og, z, gta, scf, shf, gdn_norm_w, w_out_bf, norm_ffn_w, wr_pad, br_pad, tt, heads, grouped):
    n, t, d = x.shape
    slab = d // LANES
    per_token = gta.shape[1] != 1
    mod_spec = (pl.BlockSpec((1, tt, d), lambda i, j: (i, j, 0)) if per_token
                else pl.BlockSpec((1, 1, d), lambda i, j: (i, 0, 0)))

    def tok(width):
        return pl.BlockSpec((1, tt, width), lambda i, j: (i, j, 0))

    def full(a):
        return pl.BlockSpec(a.shape, lambda i, j: (0,) * a.ndim)

    gw = gdn_norm_w.reshape(1, -1)
    nf = norm_ffn_w.reshape(1, d)
    if grouped:
        out_specs = [tok(d), pl.BlockSpec((1, tt * slab, LANES), lambda i, j: (i, j, 0)), tok(LANES), tok(LANES)]
        out_shape = [jax.ShapeDtypeStruct((n, t, d), F32), jax.ShapeDtypeStruct((n, t * slab, LANES), F32),
                     jax.ShapeDtypeStruct((n, t, LANES), F32), jax.ShapeDtypeStruct((n, t, LANES), jnp.int32)]
    else:
        out_specs = [tok(d), tok(d), tok(LANES)]
        out_shape = [jax.ShapeDtypeStruct((n, t, d), F32), jax.ShapeDtypeStruct((n, t, d), BF16),
                     jax.ShapeDtypeStruct((n, t, LANES), F32)]
    return pl.pallas_call(
        functools.partial(_mid_kernel, heads=heads, grouped=grouped),
        grid=(n, t // tt),
        in_specs=[tok(d), tok(attn.shape[2]), tok(og.shape[2]), tok(z.shape[2]), mod_spec, mod_spec, mod_spec,
                  full(gw), full(w_out_bf), full(nf), full(wr_pad), full(br_pad)],
        out_specs=out_specs,
        out_shape=out_shape,
        compiler_params=_cparams(("parallel", "parallel")),
        name="out_proj_router_grouped" if grouped else "out_proj_router",
    )(x, attn, og, z, gta, scf, shf, gw, w_out_bf, nf, wr_pad, br_pad)


def _expert_ffn(h, wup_ref, bup_ref, wdn_ref, bdn_ref, ff_chunk):
    d_ff = wdn_ref.shape[1]
    y = jnp.zeros((h.shape[0], wdn_ref.shape[2]), F32)
    for c in range(d_ff // ff_chunk):
        lo, hi = c * ff_chunk, (c + 1) * ff_chunk
        x_glu = _dot(h, wup_ref[0, :, lo:hi]) + bup_ref[0, :, lo:hi]
        x_lin = _dot(h, wup_ref[0, :, d_ff + lo:d_ff + hi]) + bup_ref[0, :, d_ff + lo:d_ff + hi]
        x_glu = jnp.minimum(x_glu, SWIGLU_LIMIT)
        x_lin = jnp.clip(x_lin, -SWIGLU_LIMIT, SWIGLU_LIMIT)
        act = x_glu * jax.nn.sigmoid(SWIGLU_ALPHA * x_glu) * (x_lin + 1.0)
        y = y + _dot(act.astype(BF16), wdn_ref[0, lo:hi, :])
    return y + bdn_ref[0]


def _moe_kernel(h2_ref, gates_ref, x1_ref, gtf_ref, wup_ref, bup_ref, wdn_ref, bdn_ref, nw_ref,
                y_ref, acc_ref, *, ff_chunk):
    e = pl.program_id(2)
    n_exp = pl.num_programs(2)

    @pl.when(e == 0)
    def _():
        acc_ref[...] = jnp.zeros(acc_ref.shape, F32)

    gates = gates_ref[0]
    lane = lax.broadcasted_iota(jnp.int32, gates.shape, 1)
    gcol = jnp.sum(jnp.where(lane == e, gates, 0.0), axis=-1, keepdims=True)
    acc_ref[...] += gcol * _expert_ffn(h2_ref[0], wup_ref, bup_ref, wdn_ref, bdn_ref, ff_chunk)

    @pl.when(e == n_exp - 1)
    def _():
        x = x1_ref[0] + gtf_ref[0] * acc_ref[...]
        y_ref[0] = x * lax.rsqrt(jnp.mean(x * x, axis=-1, keepdims=True) + EPS) * nw_ref[...]


def _moe(h2, gates, x1, gtf, w_up_bf, b_up, w_dn_bf, b_dn, norm_final_w, tm):
    n, t, d = x1.shape
    n_exp, _, ff2 = w_up_bf.shape
    d_ff = ff2 // 2
    per_token = gtf.shape[1] != 1
    mod_spec = (pl.BlockSpec((1, tm, d), lambda i, j, e: (i, j, 0)) if per_token
                else pl.BlockSpec((1, 1, d), lambda i, j, e: (i, 0, 0)))

    def tok(width):
        return pl.BlockSpec((1, tm, width), lambda i, j, e: (i, j, 0))

    return pl.pallas_call(
        functools.partial(_moe_kernel, ff_chunk=min(512, d_ff)),
        grid=(n, t // tm, n_exp),
        in_specs=[tok(d), tok(LANES), tok(d), mod_spec,
                  pl.BlockSpec((1, d, ff2), lambda i, j, e: (e, 0, 0)),
                  pl.BlockSpec((1, 1, ff2), lambda i, j, e: (e, 0, 0)),
                  pl.BlockSpec((1, d_ff, d), lambda i, j, e: (e, 0, 0)),
                  pl.BlockSpec((1, 1, d), lambda i, j, e: (e, 0, 0)),
                  pl.BlockSpec((1, d), lambda i, j, e: (0, 0))],
        out_specs=tok(d),
        out_shape=jax.ShapeDtypeStruct((n, t, d), F32),
        scratch_shapes=[pltpu.VMEM((tm, d), F32)],
        compiler_params=_cparams(("parallel", "parallel", "arbitrary")),
        name="moe_ffn",
    )(h2, gates, x1, gtf, w_up_bf, b_up.reshape(n_exp, 1, ff2), w_dn_bf, b_dn.reshape(n_exp, 1, d),
      norm_final_w.reshape(1, d))


MOE_GROUP = 4096
MOE_CHUNK = 256
MOE_SCATTER_BATCH = 8


def _moe_grouped_kernel(cnt_ref, off_ref, tok_ref, wgt_ref, xg_ref, wup_ref, bup_ref, wdn_ref, bdn_ref,
                        y_ref, xbuf, obuf, *, group, chunk, ff_chunk):
    g = pl.program_id(0)
    e = pl.program_id(1)
    slab = xbuf.shape[0] // chunk

    @pl.when(e == 0)
    def _():
        y_ref[...] = jnp.zeros(y_ref.shape, F32)
        obuf[...] = jnp.zeros(obuf.shape, F32)

    seg = g * pl.num_programs(1) + e
    off = off_ref[seg]
    end = off + cnt_ref[seg]
    n_chunks = (end - off + chunk - 1) // chunk

    def gather(r0):
        for i in range(chunk):
            tok = tok_ref[0, 0, r0 + i]
            xbuf[pl.ds(i * slab, slab), :] = xg_ref[0, pl.ds(pl.multiple_of(tok * slab, slab), slab), :]

    def scatter(r0, live):
        lim = jnp.where(live, end, 0)
        for b0 in range(0, chunk, MOE_SCATTER_BATCH):
            rows, wgts = [], []
            for i in range(b0, b0 + MOE_SCATTER_BATCH):
                valid = r0 + i < lim
                tok = jnp.where(valid, tok_ref[0, 0, r0 + i], group)
                rows.append(pl.multiple_of(tok * slab, slab))
                wgts.append(jnp.where(valid, wgt_ref[0, 0, r0 + i], 0.0))
            cur = [y_ref[0, pl.ds(r, slab), :] for r in rows]
            new = [cur[k] + wgts[k] * obuf[pl.ds((b0 + k) * slab, slab), :] for k in range(MOE_SCATTER_BATCH)]
            for k in range(MOE_SCATTER_BATCH):
                y_ref[0, pl.ds(rows[k], slab), :] = new[k]

    @pl.when(n_chunks > 0)
    def _():
        gather(off)

    def one_pass(j, carry):
        r0 = off + j * chunk
        x = jnp.concatenate([xbuf[pl.ds(s, chunk, stride=slab), :] for s in range(slab)], axis=1)
        gather(r0 + chunk)
        scatter(r0 - chunk, j > 0)
        y = _expert_ffn(x.astype(BF16), wup_ref, bup_ref, wdn_ref, bdn_ref, ff_chunk)
        for s in range(slab):
            obuf[pl.ds(s, chunk, stride=slab), :] = y[:, s * LANES:(s + 1) * LANES]
        return carry

    lax.fori_loop(0, n_chunks, one_pass, 0)

    @pl.when(n_chunks > 0)
    def _():
        scatter(off + (n_chunks - 1) * chunk, True)


def _moe_grouped(h2_slabs, topw, topi, w_up_bf, b_up, w_dn_bf, b_dn, group):
    m = topw.shape[0]
    slab = h2_slabs.shape[0] // m
    n_exp, d, ff2 = w_up_bf.shape
    d_ff = ff2 // 2
    ngroups = m // group
    rows = group * TOP_K
    eid = topi[:, :TOP_K].reshape(ngroups, rows)
    order = jnp.argsort(eid, axis=1, stable=True)
    tok_sorted = (order // TOP_K).astype(jnp.int32)
    wgt_sorted = jnp.take_along_axis(topw[:, :TOP_K].reshape(ngroups, rows), order, axis=1)
    counts = jnp.sum((eid[:, :, None] == jnp.arange(n_exp, dtype=jnp.int32)).astype(jnp.int32), axis=1)
    offs = jnp.cumsum(counts, axis=1) - counts + MOE_CHUNK
    table = rows + 3 * MOE_CHUNK
    pad = ((0, 0), (MOE_CHUNK, 2 * MOE_CHUNK))
    tok_sorted = jnp.pad(tok_sorted, pad).reshape(ngroups, 1, table)
    wgt_sorted = jnp.pad(wgt_sorted, pad).reshape(ngroups, 1, table)

    grid_spec = pltpu.PrefetchScalarGridSpec(
        num_scalar_prefetch=2,
        grid=(ngroups, n_exp),
        in_specs=[
            pl.BlockSpec((1, 1, table), lambda g, e, c, o: (g, 0, 0), memory_space=pltpu.SMEM),
            pl.BlockSpec((1, 1, table), lambda g, e, c, o: (g, 0, 0), memory_space=pltpu.SMEM),
            pl.BlockSpec((1, group * slab, LANES), lambda g, e, c, o: (g, 0, 0), pipeline_mode=pl.Buffered(1)),
            pl.BlockSpec((1, d, ff2), lambda g, e, c, o: (e, 0, 0)),
            pl.BlockSpec((1, 1, ff2), lambda g, e, c, o: (e, 0, 0)),
            pl.BlockSpec((1, d_ff, d), lambda g, e, c, o: (e, 0, 0)),
            pl.BlockSpec((1, 1, d), lambda g, e, c, o: (e, 0, 0)),
        ],
        out_specs=pl.BlockSpec((1, (group + 1) * slab, LANES), lambda g, e, c, o: (g, 0, 0),
                               pipeline_mode=pl.Buffered(1)),
        scratch_shapes=[pltpu.VMEM((MOE_CHUNK * slab, LANES), F32), pltpu.VMEM((MOE_CHUNK * slab, LANES), F32)],
    )
    return pl.pallas_call(
        functools.partial(_moe_grouped_kernel, group=group, chunk=MOE_CHUNK, ff_chunk=min(512, d_ff)),
        grid_spec=grid_spec,
        out_shape=jax.ShapeDtypeStruct((ngroups, (group + 1) * slab, LANES), F32),
        compiler_params=pltpu.CompilerParams(dimension_semantics=("parallel", "arbitrary"),
                                             vmem_limit_bytes=MOE_VMEM_LIMIT),
        name="moe_grouped",
    )(counts.reshape(-1), offs.reshape(-1), tok_sorted, wgt_sorted,
      h2_slabs.reshape(ngroups, group * slab, LANES), w_up_bf, b_up.reshape(n_exp, 1, ff2), w_dn_bf,
      b_dn.reshape(n_exp, 1, d))


def _moe_finish_kernel(moe_ref, x1_ref, gtf_ref, nw_ref, y_ref):
    tt, d = x1_ref.shape[1], x1_ref.shape[2]
    slab = d // LANES
    moe = jnp.concatenate([moe_ref[0, pl.ds(s, tt, stride=slab), :] for s in range(slab)], axis=1)
    x = x1_ref[0] + gtf_ref[0] * moe
    y_ref[0] = x * lax.rsqrt(jnp.mean(x * x, axis=-1, keepdims=True) + EPS) * nw_ref[...]


def _moe_finish(moe_slabs, x1, gtf, norm_final_w, group, tt):
    n, t, d = x1.shape
    slab = d // LANES
    per_group = group // tt
    steps = t // tt
    return pl.pallas_call(
        _moe_finish_kernel,
        grid=(n, steps),
        in_specs=[
            pl.BlockSpec((1, tt * slab, LANES),
                         lambda i, j: ((i * steps + j) // per_group, (i * steps + j) % per_group, 0)),
            pl.BlockSpec((1, tt, d), lambda i, j: (i, j, 0)),
            pl.BlockSpec((1, 1, d), lambda i, j: (i, 0, 0)),
            pl.BlockSpec((1, d), lambda i, j: (0, 0)),
        ],
        out_specs=pl.BlockSpec((1, tt, d), lambda i, j: (i, j, 0)),
        out_shape=jax.ShapeDtypeStruct((n, t, d), F32),
        compiler_params=_cparams(("parallel", "parallel")),
        name="moe_finish",
    )(moe_slabs, x1, gtf, norm_final_w.reshape(1, d))


def _pick_tile(t, pref):
    tile = min(t, pref)
    assert t % tile == 0, (t, tile)
    return tile


def kernel(x_prompt, x_sample, cache_k, cache_v, state_conv, state_ssm, page_table, c_prompt, c_sample, w_ada, b_ada, norm_attn_w, norm_ffn_w, norm_final_w, w_in, rel_bias, conv_w, a_log, dt_bias, gdn_norm_w, w_out, w_router, b_router, w_up, b_up, w_down, b_down):
    depth = w_in.shape[0]
    assert depth == 1, "single-layer step"
    nb, seq, d = x_prompt.shape
    db, dec_seq, _ = x_sample.shape
    n_pool, page, kv_heads, dh = cache_k.shape[1:]
    assert dh == ATT_HEAD_DIM
    att_heads = rel_bias.shape[1]
    assert att_heads == kv_heads * ATT_GROUP
    gdn_heads = a_log.shape[1]
    n_pages = page_table.shape[1]
    past = n_pages * page
    q_w = att_heads * dh
    kv_w = kv_heads * dh
    gdn_cw = conv_w.shape[2]
    z_w = gdn_heads * GDN_DK
    assert gdn_cw == 3 * z_w
    in_w = q_w + 2 * kv_w + gdn_cw + z_w + 2 * gdn_heads
    assert w_in.shape[2] == in_w and in_w - 2 * gdn_heads == (in_w // LANES) * LANES
    assert seq % MOBA_BLOCK == 0 and past % MOBA_BLOCK == 0 and MOBA_BLOCK % page == 0
    assert dec_seq <= MOBA_BLOCK and dec_seq >= GDN_CONV - 1 and dec_seq <= SUBLANES
    n_exp = w_router.shape[2]
    nblk = seq // MOBA_BLOCK
    widths = (q_w, kv_w, gdn_cw, z_w)

    n_mod = nb + db
    n_mod_pad = -(-n_mod // SUBLANES) * SUBLANES
    c_all = jnp.pad(jnp.concatenate([c_prompt, c_sample], axis=0), ((0, n_mod_pad - n_mod), (0, 0)))
    mod = _ada(c_all, w_ada[0], b_ada[0])
    mod_p = [mod[:nb, i * d:(i + 1) * d].reshape(nb, 1, d) for i in range(6)]
    m_s = db * dec_seq
    mod_s = [jnp.broadcast_to(mod[nb:n_mod, None, i * d:(i + 1) * d], (db, dec_seq, d)).reshape(1, m_s, d)
             for i in range(6)]

    w_in_pad = jnp.pad(w_in[0], ((0, 0), (0, LANES - 2 * gdn_heads))).astype(BF16)
    w_out_bf = w_out[0].astype(BF16)
    wr_pad = jnp.pad(w_router[0], ((0, 0), (0, LANES - n_exp)))
    br_pad = jnp.pad(b_router[0], (0, LANES - n_exp), constant_values=NEG).reshape(1, LANES)
    w_up_bf = w_up[0].astype(BF16)
    w_dn_bf = w_down[0].astype(BF16)

    tt_p = _pick_tile(seq, 512)
    q_p, k_p, v_p, conv_p, z_p, ba_p, kh_p, vt_p, km_p = _inproj(
        x_prompt, mod_p[0], mod_p[1], norm_attn_w[0], w_in_pad, widths, tt_p, True)
    km_h = km_p.reshape(nb, nblk, kv_heads, dh).transpose(0, 2, 1, 3)
    bias_t = _bias_tiles(rel_bias, nblk)
    attn_p = _attn_prompt(q_p, kh_p, vt_p, km_h, bias_t)
    og_p, ssm_p = _gdn_prompt(conv_p, ba_p, conv_w[0], a_log[0], dt_bias[0], gdn_heads, _pick_tile(seq, 256))
    x1_p, h2_p, topw_p, topi_p = _mid(x_prompt, attn_p, og_p, z_p, mod_p[2], mod_p[4], mod_p[3], gdn_norm_w[0],
                                      w_out_bf, norm_ffn_w[0], wr_pad, br_pad, tt_p, gdn_heads, True)
    m_p = nb * seq
    group = _pick_tile(m_p, MOE_GROUP)
    assert group % tt_p == 0 and seq % tt_p == 0
    moe_p = _moe_grouped(h2_p.reshape(m_p * (d // LANES), LANES), topw_p.reshape(m_p, LANES),
                         topi_p.reshape(m_p, LANES), w_up_bf, b_up[0], w_dn_bf, b_down[0], group)
    y_p = _moe_finish(moe_p, x1_p, mod_p[5], norm_final_w, group, tt_p)

    tt_s = _pick_tile(m_s, 512)
    q_s, k_s, v_s, conv_s, z_s, ba_s = _inproj(
        x_sample.reshape(1, m_s, d), mod_s[0], mod_s[1], norm_attn_w[0], w_in_pad, widths, tt_s, False)
    q4 = q_s.reshape(db, dec_seq, kv_heads, ATT_GROUP, dh).transpose(0, 2, 3, 1, 4)
    q4 = q4.reshape(db, kv_heads, ATT_GROUP * dec_seq, 1, dh)
    eye = jnp.eye(kv_heads, dtype=F32).reshape(1, kv_heads, 1, kv_heads, 1)
    qc = (q4 * eye).reshape(db, att_heads * dec_seq, kv_w)
    new_pad = ((0, 0), (0, SUBLANES - dec_seq), (0, 0))
    k_new = jnp.pad(k_s.reshape(db, dec_seq, kv_w), new_pad)
    v_new = jnp.pad(v_s.reshape(db, dec_seq, kv_w), new_pad)
    bias_r = _bias_rows(rel_bias, past, dec_seq, past + MOBA_BLOCK)
    ck_t = cache_k[0].transpose(0, 2, 3, 1).reshape(n_pool, kv_w, page)
    cv_t = cache_v[0].transpose(0, 2, 3, 1).reshape(n_pool, kv_w, page)
    attn_rows = _attn_sample(page_table, qc, k_new, v_new, bias_r, ck_t, cv_t, dec_seq, kv_heads)
    attn_s = attn_rows.reshape(db, att_heads, dec_seq, dh).transpose(0, 2, 1, 3).reshape(1, m_s, q_w)

    conv_s3 = conv_s.reshape(db, dec_seq, gdn_cw)
    ext = jnp.concatenate([state_conv[0], conv_s3], axis=1)
    gs = _pick_tile(db, SUBLANES)
    og_tm, ssm_s = _gdn_sample(jnp.swapaxes(ext, 0, 1), jnp.swapaxes(ba_s.reshape(db, dec_seq, LANES), 0, 1),
                               conv_w[0], a_log[0], dt_bias[0], state_ssm[0], gdn_heads, gs)
    og_s = jnp.swapaxes(og_tm, 0, 1).reshape(1, m_s, z_w)
    x1_s, h2_s, gates_s = _mid(x_sample.reshape(1, m_s, d), attn_s, og_s, z_s, mod_s[2], mod_s[4], mod_s[3],
                               gdn_norm_w[0], w_out_bf, norm_ffn_w[0], wr_pad, br_pad, tt_s, gdn_heads, False)
    y_s = _moe(h2_s, gates_s, x1_s, mod_s[5], w_up_bf, b_up[0], w_dn_bf, b_down[0], norm_final_w, tt_s)

    return (
        y_p,
        y_s.reshape(db, dec_seq, d),
        k_p.reshape(1, nb, seq, kv_heads, dh),
        v_p.reshape(1, nb, seq, kv_heads, dh),
        conv_p[:, seq - (GDN_CONV - 1):, :][None],
        ssm_p[None],
        k_s.reshape(1, db, dec_seq, kv_heads, dh),
        v_s.reshape(1, db, dec_seq, kv_heads, dh),
        ext[:, dec_seq:, :][None],
        ssm_s[None],
    )
```
